```python
import math
import jax
import jax.numpy as jnp
from jax import lax
import numpy as np

D_MODEL = 2048
BATCH = 8
SEQ = 2048
DEPTH = 2

CTX_LEN = 256
GRID_W = 64
HEAD_DIM = 128
Q_BLOCK = 128
ROPE_THETA = 10000.0
NORM_EPS = 1e-6
LN_EPS = 1e-5
MIX_HALF = D_MODEL // 2

CONV_CH = MIX_HALF
CONV_WIDTH = 31
GQA_Q_HEADS = MIX_HALF // HEAD_DIM
GQA_KV_HEADS = max(1, GQA_Q_HEADS // 4)
GQA_GROUP = GQA_Q_HEADS // GQA_KV_HEADS
GQA_Q = GQA_Q_HEADS * HEAD_DIM
GQA_KV = GQA_KV_HEADS * HEAD_DIM
IN_EVEN = 2 * CONV_CH + GQA_Q + 2 * GQA_KV
MIX_EVEN = CONV_CH + GQA_Q

HYENA_CH = MIX_HALF
HYENA_ORDER = 2
HYENA_SHORT = 3
HYENA_POS_DIM = 33
HYENA_FILTER_WIDTH = 64
HYENA_FAST_DECAY = 0.3
HYENA_SLOW_DECAY = 1.5
HYENA_TARGET = 1e-2
HYENA_MAX_DECAY = math.log(HYENA_TARGET) / HYENA_FAST_DECAY
HYENA_MIN_DECAY = math.log(HYENA_TARGET) / HYENA_SLOW_DECAY
HYENA_IN = (HYENA_ORDER + 1) * HYENA_CH
MLA_HEADS = MIX_HALF // HEAD_DIM
MLA_Q_RANK = D_MODEL // 4
MLA_KV_RANK = D_MODEL // 8
MLA_NOPE = HEAD_DIM
MLA_ROPE = HEAD_DIM // 2
MLA_V = HEAD_DIM
MLA_QK = MLA_NOPE + MLA_ROPE
IN_ODD = HYENA_IN + MLA_Q_RANK + MLA_KV_RANK + MLA_ROPE
MIX_ODD = HYENA_CH + MLA_HEADS * MLA_V

FFN_RAW = -(-8 * D_MODEL // 3)
FFN_HIDDEN = -(-FFN_RAW // 256) * 256

kernel_name = 'hybrid_diffusion_conv_gqa_hyena_mla'


def rms_norm(x, g, eps=NORM_EPS):
    xf = x.astype(jnp.float32)
    y = xf * lax.rsqrt(jnp.mean(xf * xf, axis=-1, keepdims=True) + eps)
    return (y * g.astype(jnp.float32)).astype(x.dtype)


def layer_norm(x, g, b, eps=LN_EPS):
    xf = x.astype(jnp.float32)
    mu = jnp.mean(xf, axis=-1, keepdims=True)
    xc = xf - mu
    var = jnp.mean(xc * xc, axis=-1, keepdims=True)
    y = xc * lax.rsqrt(var + eps) * g.astype(jnp.float32) + b.astype(jnp.float32)
    return y.astype(x.dtype)


def modulate(h, shift, scale):
    return h * (1 + scale) + shift


def rope_1d(x, pos):
    d = x.shape[-1]
    half = d // 2
    inv = ROPE_THETA ** (-jnp.arange(half, dtype=jnp.float32) / half)
    ang = pos[:, None] * inv[None, :]
    shape = (pos.shape[0],) + (1,) * (x.ndim - 3) + (half,)
    cos = jnp.cos(ang).reshape(shape)
    sin = jnp.sin(ang).reshape(shape)
    xf = x.astype(jnp.float32)
    x1, x2 = xf[..., :half], xf[..., half:]
    return jnp.concatenate([x1 * cos - x2 * sin, x1 * sin + x2 * cos], axis=-1).astype(x.dtype)


def axial_rope(x, row, col):
    h = x.shape[-1] // 2
    return jnp.concatenate([rope_1d(x[..., :h], row), rope_1d(x[..., h:], col)], axis=-1)


def blocked_attention(q, k, v, scale):
    B, L, Hk, G, dk = q.shape
    nb = L // Q_BLOCK
    qb = jnp.moveaxis(q.reshape(B, nb, Q_BLOCK, Hk, G, dk), 1, 0)

    def one_block(qi):
        s = jnp.einsum('bqhgd,bthd->bhgqt', qi, k, preferred_element_type=jnp.float32) * scale
        p = jax.nn.softmax(s, axis=-1).astype(v.dtype)
        return jnp.einsum('bhgqt,bthd->bqhgd', p, v)

    out = lax.map(one_block, qb)
    return jnp.moveaxis(out, 0, 1).reshape(B, L, Hk, G, v.shape[-1])


def depthwise_conv(x, w, b):
    K, C = w.shape
    pad = (K - 1) // 2
    y = lax.conv_general_dilated(x, w.astype(x.dtype)[:, None, :], window_strides=(1,),
                                 padding=[(pad, pad)], dimension_numbers=('NWC', 'WIO', 'NWC'),
                                 feature_group_count=C)
    return y + b.astype(x.dtype)


def swiglu(h, wg, wu, wd):
    return (jax.nn.silu(h @ wg) * (h @ wu)) @ wd


def conformer_conv(z, dw_w, dw_b, ln_g, ln_b):
    a, g = jnp.split(z, 2, axis=-1)
    u = a * jax.nn.sigmoid(g)
    u = depthwise_conv(u, dw_w, dw_b)
    return jax.nn.silu(layer_norm(u, ln_g, ln_b))


def gqa_q(zq, qn_g):
    B, L = zq.shape[:2]
    return rms_norm(zq.reshape(B, L, GQA_KV_HEADS, GQA_GROUP, HEAD_DIM), qn_g)


def gqa_kv(zkv, kn_g):
    B, L = zkv.shape[:2]
    k = rms_norm(zkv[..., :GQA_KV].reshape(B, L, GQA_KV_HEADS, HEAD_DIM), kn_g)
    v = zkv[..., GQA_KV:].reshape(B, L, GQA_KV_HEADS, HEAD_DIM)
    return k, v


def even_mixer(h, hc, w_in, w_out, dw_w, dw_b, ln_g, ln_b, qn_g, kn_g, row, col, ctx_out):
    B, L, _ = h.shape
    Lc = hc.shape[1]
    q0 = 2 * CONV_CH
    kv0 = q0 + GQA_Q
    scale = HEAD_DIM ** -0.5
    z = h @ w_in
    q = axial_rope(gqa_q(z[..., q0:kv0], qn_g), row, col)
    k, v = gqa_kv(z[..., kv0:], kn_g)
    k = axial_rope(k, row, col)
    kc, vc = gqa_kv(hc @ w_in[:, kv0:], kn_g)
    att = blocked_attention(q, jnp.concatenate([kc, k], axis=1), jnp.concatenate([vc, v], axis=1), scale)
    conv = conformer_conv(z[..., :q0], dw_w, dw_b, ln_g, ln_b)
    out = jnp.concatenate([conv, att.reshape(B, L, GQA_Q)], axis=-1) @ w_out
    if not ctx_out:
        return out, None
    zc = hc @ w_in[:, :kv0]
    att_c = blocked_attention(gqa_q(zc[..., q0:], qn_g), kc, vc, scale)
    conv_c = conformer_conv(zc[..., :q0], dw_w, dw_b, ln_g, ln_b)
    out_c = jnp.concatenate([conv_c, att_c.reshape(B, Lc, GQA_Q)], axis=-1) @ w_out
    return out, out_c


def hyena_filters(L, w1, b1, w2, b2, w3, b3, w4, freq):
    f32 = jnp.float32
    t = jnp.linspace(0.0, 1.0, L, dtype=f32)[:, None]
    bands = (HYENA_POS_DIM - 1) // 2
    w = (2.0 * math.pi / L) * jnp.arange(L, dtype=f32)[:, None]
    f = jnp.linspace(1e-4, bands - 1, bands, dtype=f32)[None, :]
    feat = jnp.concatenate([t, jnp.cos(f * w), -jnp.sin(f * w)], axis=-1)
    fr = freq.astype(f32)
    hh = jnp.sin(fr[0] * (feat @ w1.astype(f32) + b1.astype(f32)))
    hh = jnp.sin(fr[1] * (hh @ w2.astype(f32) + b2.astype(f32)))
    hh = jnp.sin(fr[2] * (hh @ w3.astype(f32) + b3.astype(f32)))
    hh = (hh @ w4.astype(f32)).reshape(L, 2, HYENA_ORDER, HYENA_CH)
    deltas = jnp.abs(jnp.linspace(HYENA_MIN_DECAY, HYENA_MAX_DECAY, HYENA_CH, dtype=f32))
    hh = hh * jnp.exp(-t * deltas[None, :])[:, None, None, :]
    return hh / (jnp.sum(jnp.abs(hh), axis=0, keepdims=True) + 1e-6)


def bidir_long_conv(u, h_fwd, h_bwd, skip):
    L, C = h_fwd.shape
    k = jnp.concatenate([h_fwd, jnp.zeros((1, C), h_fwd.dtype), h_bwd[:0:-1]], axis=0)
    kf = jnp.fft.rfft(k, axis=0)
    uf32 = u.astype(jnp.float32)
    uf = jnp.fft.rfft(uf32, n=2 * L, axis=1)
    y = jnp.fft.irfft(uf * kf[None], n=2 * L, axis=1)[:, :L]
    return (y + uf32 * skip.astype(jnp.float32)).astype(u.dtype)


def hyena_mix(z, short_w, short_b, filt, skip):
    L = z.shape[1]
    z = depthwise_conv(z, short_w, short_b)
    parts = jnp.split(z, HYENA_ORDER + 1, axis=-1)
    hh = hyena_filters(L, *filt)
    y = parts[-1]
    for n in range(HYENA_ORDER):
        y = parts[n] * bidir_long_conv(y, hh[:, 0, n], hh[:, 1, n], skip[n])
    return y


def mla_q(zq, qn_g, w_uq, row, col):
    B, L = zq.shape[:2]
    q = (rms_norm(zq, qn_g) @ w_uq).reshape(B, L, MLA_HEADS, MLA_QK)
    q_nope, q_rope = q[..., :MLA_NOPE], q[..., MLA_NOPE:]
    if row is not None:
        q_rope = axial_rope(q_rope, row, col)
    return jnp.concatenate([q_nope, q_rope], axis=-1)[:, :, :, None, :]


def mla_kv(zkv, kvn_g, w_ukv, row, col):
    B, L = zkv.shape[:2]
    ckv = rms_norm(zkv[..., :MLA_KV_RANK], kvn_g)
    k_rope = zkv[..., MLA_KV_RANK:]
    if row is not None:
        k_rope = axial_rope(k_rope, row, col)
    kv = (ckv @ w_ukv).reshape(B, L, MLA_HEADS, MLA_NOPE + MLA_V)
    k_nope, v = kv[..., :MLA_NOPE], kv[..., MLA_NOPE:]
    k = jnp.concatenate([k_nope, jnp.broadcast_to(k_rope[:, :, None, :], (B, L, MLA_HEADS, MLA_ROPE))], axis=-1)
    return k, v


def odd_mixer(h, hc, w_in, w_out, short_w, short_b, filt, skip, qn_g, kvn_g, w_uq, w_ukv, row, col, ctx_out):
    B, L, _ = h.shape
    Lc = hc.shape[1]
    kv0 = HYENA_IN + MLA_Q_RANK
    scale = MLA_QK ** -0.5
    z = h @ w_in
    y_h = hyena_mix(z[..., :HYENA_IN], short_w, short_b, filt, skip)
    q = mla_q(z[..., HYENA_IN:kv0], qn_g, w_uq, row, col)
    k, v = mla_kv(z[..., kv0:], kvn_g, w_ukv, row, col)
    kc, vc = mla_kv(hc @ w_in[:, kv0:], kvn_g, w_ukv, None, None)
    att = blocked_attention(q, jnp.concatenate([kc, k], axis=1), jnp.concatenate([vc, v], axis=1), scale)
    out = jnp.concatenate([y_h, att.reshape(B, L, MLA_HEADS * MLA_V)], axis=-1) @ w_out
    if not ctx_out:
        return out, None
    zc = hc @ w_in[:, :kv0]
    y_hc = hyena_mix(zc[..., :HYENA_IN], short_w, short_b, filt, skip)
    att_c = blocked_attention(mla_q(zc[..., HYENA_IN:], qn_g, w_uq, None, None), kc, vc, scale)
    out_c = jnp.concatenate([y_hc, att_c.reshape(B, Lc, MLA_HEADS * MLA_V)], axis=-1) @ w_out
    return out, out_c


def setup_inputs(seed: int = 0) -> dict:
    key = jax.random.key(seed)
    ks = iter(jax.random.split(key, 64))
    D = D_MODEL
    n_even = (DEPTH + 1) // 2
    n_odd = DEPTH // 2
    F = HYENA_FILTER_WIDTH

    def nrm(shape, scale):
        return scale * jax.random.normal(next(ks), shape, jnp.float32)

    def gain(shape):
        return 1.0 + nrm(shape, 0.01)

    return {
        'x': nrm((BATCH, SEQ, D), 1.0),
        'c': nrm((BATCH, D), 1.0),
        'ctx': nrm((BATCH, CTX_LEN, D), 1.0),
        'c_ctx': nrm((D,), 1.0),
        'ada_w': nrm((DEPTH, D, 6 * D), D ** -0.5),
        'ada_b': nrm((DEPTH, 6 * D), 0.01),
        'norm_mix_g': gain((DEPTH, D)),
        'norm_ffn_g': gain((DEPTH, D)),
        'e_w_in': nrm((n_even, D, IN_EVEN), D ** -0.5),
        'e_w_out': nrm((n_even, MIX_EVEN, D), MIX_EVEN ** -0.5),
        'e_dw_w': nrm((n_even, CONV_WIDTH, CONV_CH), CONV_WIDTH ** -0.5),
        'e_dw_b': nrm((n_even, CONV_CH), 0.01),
        'e_ln_g': gain((n_even, CONV_CH)),
        'e_ln_b': nrm((n_even, CONV_CH), 0.01),
        'e_qn_g': gain((n_even, HEAD_DIM)),
        'e_kn_g': gain((n_even, HEAD_DIM)),
        'o_w_in': nrm((n_odd, D, IN_ODD), D ** -0.5),
        'o_w_out': nrm((n_odd, MIX_ODD, D), MIX_ODD ** -0.5),
        'o_short_w': nrm((n_odd, HYENA_SHORT, HYENA_IN), HYENA_SHORT ** -0.5),
        'o_short_b': nrm((n_odd, HYENA_IN), 0.01),
        'o_f_w1': nrm((n_odd, HYENA_POS_DIM, F), HYENA_POS_DIM ** -0.5),
        'o_f_b1': nrm((n_odd, F), 0.01),
        'o_f_w2': nrm((n_odd, F, F), F ** -0.5),
        'o_f_b2': nrm((n_odd, F), 0.01),
        'o_f_w3': nrm((n_odd, F, F), F ** -0.5),
        'o_f_b3': nrm((n_odd, F), 0.01),
        'o_f_w4': nrm((n_odd, F, 2 * HYENA_ORDER * HYENA_CH), F ** -0.5),
        'o_f_freq': gain((n_odd, 3, F)),
        'o_skip': nrm((n_odd, HYENA_ORDER, HYENA_CH), 0.5),
        'o_q_norm_g': gain((n_odd, MLA_Q_RANK)),
        'o_kv_norm_g': gain((n_odd, MLA_KV_RANK)),
        'o_w_uq': nrm((n_odd, MLA_Q_RANK, MLA_HEADS * MLA_QK), MLA_Q_RANK ** -0.5),
        'o_w_ukv': nrm((n_odd, MLA_KV_RANK, MLA_HEADS * (MLA_NOPE + MLA_V)), MLA_KV_RANK ** -0.5),
        'ffn_w_gate': nrm((DEPTH, D, FFN_HIDDEN), D ** -0.5),
        'ffn_w_up': nrm((DEPTH, D, FFN_HIDDEN), D ** -0.5),
        'ffn_w_down': nrm((DEPTH, FFN_HIDDEN, D), FFN_HIDDEN ** -0.5),
        'final_norm_g': gain((D,)),
    }


def reference(x, c, ctx, c_ctx, ada_w, ada_b, norm_mix_g, norm_ffn_g,
              e_w_in, e_w_out, e_dw_w, e_dw_b, e_ln_g, e_ln_b, e_qn_g, e_kn_g,
              o_w_in, o_w_out, o_short_w, o_short_b, o_f_w1, o_f_b1, o_f_w2, o_f_b2,
              o_f_w3, o_f_b3, o_f_w4, o_f_freq, o_skip, o_q_norm_g, o_kv_norm_g, o_w_uq, o_w_ukv,
              ffn_w_gate, ffn_w_up, ffn_w_down, final_norm_g):
    S = x.shape[1]
    ROWS = S // GRID_W
    row = jnp.repeat(jnp.arange(ROWS, dtype=jnp.float32), GRID_W)
    col = jnp.tile(jnp.arange(GRID_W, dtype=jnp.float32), ROWS)
    cond = jax.nn.silu(c)
    cond_ctx = jax.nn.silu(c_ctx)
    for i in range(DEPTH):
        ctx_out = i < DEPTH - 1
        mod = cond @ ada_w[i] + ada_b[i]
        mod_c = cond_ctx @ ada_w[i] + ada_b[i]
        sm, scm, gm, sf, scf, gf = jnp.split(mod[:, None, :], 6, axis=-1)
        csm, cscm, cgm, csf, cscf, cgf = jnp.split(mod_c[None, None, :], 6, axis=-1)
        h = modulate(rms_norm(x, norm_mix_g[i]), sm, scm)
        hc = modulate(rms_norm(ctx, norm_mix_g[i]), csm, cscm)
        j = i // 2
        if i % 2 == 0:
            o, oc = even_mixer(h, hc, e_w_in[j], e_w_out[j], e_dw_w[j], e_dw_b[j], e_ln_g[j], e_ln_b[j],
                               e_qn_g[j], e_kn_g[j], row, col, ctx_out)
        else:
            filt = (o_f_w1[j], o_f_b1[j], o_f_w2[j], o_f_b2[j], o_f_w3[j], o_f_b3[j], o_f_w4[j], o_f_freq[j])
            o, oc = odd_mixer(h, hc, o_w_in[j], o_w_out[j], o_short_w[j], o_short_b[j], filt, o_skip[j],
                              o_q_norm_g[j], o_kv_norm_g[j], o_w_uq[j], o_w_ukv[j], row, col, ctx_out)
        x = x + gm * o
        x = x + gf * swiglu(modulate(rms_norm(x, norm_ffn_g[i]), sf, scf), ffn_w_gate[i], ffn_w_up[i], ffn_w_down[i])
        if ctx_out:
            ctx = ctx + cgm * oc
            ctx = ctx + cgf * swiglu(modulate(rms_norm(ctx, norm_ffn_g[i]), csf, cscf),
                                     ffn_w_gate[i], ffn_w_up[i], ffn_w_down[i])
    return rms_norm(x, final_norm_g)
```

```python
import functools
import math

import numpy as np
import jax
import jax.numpy as jnp
from jax import lax
from jax.experimental import pallas as pl
from jax.experimental.pallas import tpu as pltpu

F32 = jnp.float32
BF16 = jnp.bfloat16

D_MODEL = 2048
GRID_W = 64
HEAD_DIM = 128
ROPE_THETA = 10000.0
NORM_EPS = 1e-6
LN_EPS = 1e-5
MIX_HALF = D_MODEL // 2
CONV_WIDTH = 31
CONV_PAD = (CONV_WIDTH - 1) // 2
GQA_Q_HEADS = MIX_HALF // HEAD_DIM
GQA_KV_HEADS = GQA_Q_HEADS // 4
GQA_GROUP = GQA_Q_HEADS // GQA_KV_HEADS
HYENA_CH = MIX_HALF
HYENA_ORDER = 2
HYENA_POS_DIM = 33
HYENA_POS_PAD = 40
HYENA_FAST_DECAY = 0.3
HYENA_SLOW_DECAY = 1.5
HYENA_TARGET = 1e-2
HYENA_MAX_DECAY = math.log(HYENA_TARGET) / HYENA_FAST_DECAY
HYENA_MIN_DECAY = math.log(HYENA_TARGET) / HYENA_SLOW_DECAY
HYENA_IN = 3 * HYENA_CH
MLA_HEADS = MIX_HALF // HEAD_DIM
MLA_Q_RANK = D_MODEL // 4
MLA_KV_RANK = D_MODEL // 8
MLA_NOPE = HEAD_DIM
MLA_ROPE = HEAD_DIM // 2
MLA_QK = MLA_NOPE + MLA_ROPE
MLA_QK_PAD = 2 * HEAD_DIM
IN_ODD = HYENA_IN + MLA_Q_RANK + MLA_KV_RANK + MLA_ROPE
IN_ODD_PAD = 4096

V7X_VMEM_LIMIT_BYTES = 56 * 1024 * 1024
FREQ_TILE = 512


def _params(n_axes):
    return pltpu.CompilerParams(
        dimension_semantics=("arbitrary",) * n_axes,
        vmem_limit_bytes=V7X_VMEM_LIMIT_BYTES,
    )


ROW_ALIGN = 256


def _tile(n, target, align=ROW_ALIGN):
    best = None
    for cand in range(align, min(n, target) + 1, align):
        if n % cand == 0:
            best = cand
    assert best is not None, (n, target, align)
    return best


def _dot(a, b):
    return jnp.dot(a, b, preferred_element_type=F32)


def _dot_hi(a, b):
    return jnp.dot(a, b, preferred_element_type=F32, precision=lax.Precision.HIGHEST)


def _silu(v):
    return v * jax.nn.sigmoid(v)


def _ada_kernel(c_ref, w_ref, b_ref, o_ref):
    o_ref[0] = _dot_hi(_silu(c_ref[...]), w_ref[0]) + b_ref[0]


def _ada(c_rows, ada_w, ada_b):
    depth, d, n = ada_w.shape
    r = c_rows.shape[0]
    tn = 512
    return pl.pallas_call(
        _ada_kernel,
        grid=(depth, n // tn),
        in_specs=[
            pl.BlockSpec((r, d), lambda l, j: (0, 0)),
            pl.BlockSpec((1, d, tn), lambda l, j: (l, 0, j)),
            pl.BlockSpec((1, 1, tn), lambda l, j: (l, 0, j)),
        ],
        out_specs=pl.BlockSpec((1, r, tn), lambda l, j: (l, 0, j)),
        out_shape=jax.ShapeDtypeStruct((depth, r, n), F32),
        compiler_params=_params(2),
        name="ada_mod",
    )(c_rows, ada_w, ada_b.reshape(depth, 1, n))


def _norm_mod_kernel(x_ref, g_ref, sh_ref, sc_ref, o_ref):
    x = x_ref[0]
    y = x * lax.rsqrt(jnp.mean(x * x, axis=-1, keepdims=True) + NORM_EPS) * g_ref[...]
    o_ref[0] = (y * (1.0 + sc_ref[0]) + sh_ref[0]).astype(o_ref.dtype)


def _mod_row(b, j, ctx_block, n_batch):
    if ctx_block is None:
        return b
    return jnp.where(j >= ctx_block, n_batch, b)


def _norm_mod(s, g, mod, k_shift, k_scale, rows, ctx_rows_from):
    n_batch, _, d = s.shape
    tr = 256
    ctx_block = None if ctx_rows_from is None else ctx_rows_from // tr

    def mod_spec(k):
        return pl.BlockSpec(
            (1, 1, d), lambda b, j: (_mod_row(b, j, ctx_block, n_batch) * 6 + k, 0, 0))

    return pl.pallas_call(
        _norm_mod_kernel,
        grid=(n_batch, rows // tr),
        in_specs=[
            pl.BlockSpec((1, tr, d), lambda b, j: (b, j, 0)),
            pl.BlockSpec((1, d), lambda b, j: (0, 0)),
            mod_spec(k_shift),
            mod_spec(k_scale),
        ],
        out_specs=pl.BlockSpec((1, tr, d), lambda b, j: (b, j, 0)),
        out_shape=jax.ShapeDtypeStruct((n_batch, rows, d), BF16),
        compiler_params=_params(2),
        name="norm_mod",
    )(s, g.reshape(1, d), mod, mod)


def _final_norm_kernel(x_ref, g_ref, o_ref):
    x = x_ref[0]
    o_ref[0] = x * lax.rsqrt(jnp.mean(x * x, axis=-1, keepdims=True) + NORM_EPS) * g_ref[...]


def _final_norm(s, g):
    n_batch, rows, d = s.shape
    tr = 256
    return pl.pallas_call(
        _final_norm_kernel,
        grid=(n_batch, rows // tr),
        in_specs=[pl.BlockSpec((1, tr, d), lambda b, j: (b, j, 0)),
                  pl.BlockSpec((1, d), lambda b, j: (0, 0))],
        out_specs=pl.BlockSpec((1, tr, d), lambda b, j: (b, j, 0)),
        out_shape=jax.ShapeDtypeStruct((n_batch, rows, d), F32),
        compiler_params=_params(2),
        name="final_norm",
    )(s, g.reshape(1, d))


def _mm_kernel(a_ref, w_ref, o_ref):
    o_ref[...] = _dot(a_ref[...], w_ref[...]).astype(o_ref.dtype)


def _mm(a, w, tm, tn, name):
    m, k = a.shape
    n = w.shape[1]
    tm = _tile(m, tm)
    return pl.pallas_call(
        _mm_kernel,
        grid=(n // tn, m // tm),
        in_specs=[pl.BlockSpec((tm, k), lambda j, i: (i, 0)),
                  pl.BlockSpec((k, tn), lambda j, i: (0, j))],
        out_specs=pl.BlockSpec((tm, tn), lambda j, i: (i, j)),
        out_shape=jax.ShapeDtypeStruct((m, n), BF16),
        compiler_params=_params(2),
        name=name,
    )(a, w)


def _mm_glu_kernel(a_ref, wg_ref, wu_ref, o_ref):
    a = a_ref[...]
    o_ref[...] = (_silu(_dot(a, wg_ref[...])) * _dot(a, wu_ref[...])).astype(o_ref.dtype)


def _mm_glu(a, wg, wu, tm, tn):
    m, k = a.shape
    n = wg.shape[1]
    tm = _tile(m, tm)
    return pl.pallas_call(
        _mm_glu_kernel,
        grid=(n // tn, m // tm),
        in_specs=[pl.BlockSpec((tm, k), lambda j, i: (i, 0)),
                  pl.BlockSpec((k, tn), lambda j, i: (0, j)),
                  pl.BlockSpec((k, tn), lambda j, i: (0, j))],
        out_specs=pl.BlockSpec((tm, tn), lambda j, i: (i, j)),
        out_shape=jax.ShapeDtypeStruct((m, n), BF16),
        compiler_params=_params(2),
        name="ffn_gate_up",
    )(a, wg, wu)


def _mm_resid_kernel(*refs, n_a, ctx_sub):
    a_refs = refs[:n_a]
    w_refs = refs[n_a:2 * n_a]
    r_ref, gb_ref, gc_ref, o_ref = refs[2 * n_a:]
    acc = _dot(a_refs[0][0], w_refs[0][...])
    for a_ref, w_ref in zip(a_refs[1:], w_refs[1:]):
        acc = acc + _dot(a_ref[0], w_ref[...])
    i = pl.program_id(2)
    n_sub = 1 if ctx_sub is None else acc.shape[0] // ROW_ALIGN
    sub = acc.shape[0] // n_sub
    for s in range(n_sub):
        gate = gb_ref[0]
        if ctx_sub is not None:
            gate = jnp.where(i * n_sub + s >= ctx_sub, gc_ref[0], gate)
        rows = slice(s * sub, (s + 1) * sub)
        o_ref[0, rows, :] = r_ref[0, rows, :] + gate * acc[rows, :]


def _mm_resid(a_list, w_list, resid, mod, k_gate, rows, tm, tn, ctx_rows_from, name):
    n_batch = resid.shape[0]
    n = w_list[0].shape[1]
    tm = _tile(rows, tm)
    ctx_sub = None if ctx_rows_from is None else ctx_rows_from // ROW_ALIGN
    ctx_row = n_batch
    in_specs = [pl.BlockSpec((1, tm, a.shape[2]), lambda j, b, i: (b, i, 0)) for a in a_list]
    in_specs += [pl.BlockSpec((w.shape[0], tn), lambda j, b, i: (0, j)) for w in w_list]
    in_specs += [
        pl.BlockSpec((1, tm, tn), lambda j, b, i: (b, i, j)),
        pl.BlockSpec((1, 1, tn), lambda j, b, i: (b * 6 + k_gate, 0, j)),
        pl.BlockSpec((1, 1, tn), lambda j, b, i: (ctx_row * 6 + k_gate, 0, j)),
    ]
    return pl.pallas_call(
        functools.partial(_mm_resid_kernel, n_a=len(a_list), ctx_sub=ctx_sub),
        grid=(n // tn, n_batch, rows // tm),
        in_specs=in_specs,
        out_specs=pl.BlockSpec((1, tm, tn), lambda j, b, i: (b, i, j)),
        out_shape=jax.ShapeDtypeStruct((n_batch, rows, n), F32),
        compiler_params=_params(3),
        name=name,
    )(*a_list, *w_list, resid, mod, mod)


def _axial_tables(n_lat, n_ctx, axis_dims):
    half = axis_dims // 2
    t = np.arange(n_lat)
    pos = np.stack([t // GRID_W, t % GRID_W], axis=1).astype(np.float64)
    inv = ROPE_THETA ** (-np.arange(half, dtype=np.float64) / half)
    lane = np.arange(HEAD_DIM)
    active = lane < 2 * axis_dims
    axis = np.where(active, lane // axis_dims, 0)
    within = lane % axis_dims
    ang = pos[:, axis] * inv[within % half][None, :]
    cos = np.where(active[None, :], np.cos(ang), 1.0)
    sin = np.where(active[None, :], np.sin(ang), 0.0)
    first = (within < half)[None, :]
    sin_up = np.where(first, -sin, 0.0)
    sin_dn = np.where(first, 0.0, sin)
    ident = np.zeros((n_ctx, HEAD_DIM))
    cos = np.concatenate([cos, ident + 1.0], axis=0)
    sin_up = np.concatenate([sin_up, ident], axis=0)
    sin_dn = np.concatenate([sin_dn, ident], axis=0)
    return tuple(jnp.asarray(v, dtype=F32) for v in (cos, sin_up, sin_dn))


def _rope128(v, cos, sin_up, sin_dn, half):
    return (v * cos + pltpu.roll(v, HEAD_DIM - half, 1) * sin_up
            + pltpu.roll(v, half, 1) * sin_dn)


def _softmax_pv(s, v):
    m = jnp.max(s, axis=-1, keepdims=True)
    p = jnp.exp(s - m)
    l = jnp.sum(p, axis=-1, keepdims=True)
    return _dot(p.astype(BF16), v) / l


def _nt_dot(q, k):
    return lax.dot_general(q, k, (((1,), (1,)), ((), ())), preferred_element_type=F32)


def _gqa_kernel(q_ref, k_ref, v_ref, cq_ref, uq_ref, dq_ref, ck_ref, uk_ref, dk_ref,
                qg_ref, kg_ref, o_ref, kb_ref, vb_ref, *, n_lat, ctx_block):
    j = pl.program_id(2)
    half = HEAD_DIM // 4

    @pl.when(j == 0)
    def _():
        k = k_ref[0].astype(F32)
        k = k * lax.rsqrt(jnp.mean(k * k, axis=-1, keepdims=True) + NORM_EPS) * kg_ref[...]
        kb_ref[...] = _rope128(k, ck_ref[...], uk_ref[...], dk_ref[...], half).astype(BF16)
        vb_ref[...] = v_ref[0]

    def attend(k, v):
        for g in range(GQA_GROUP):
            cols = slice(g * HEAD_DIM, (g + 1) * HEAD_DIM)
            q = q_ref[0, :, cols].astype(F32)
            q = q * lax.rsqrt(jnp.mean(q * q, axis=-1, keepdims=True) + NORM_EPS) * qg_ref[...]
            q = _rope128(q, cq_ref[...], uq_ref[...], dq_ref[...], half) * (HEAD_DIM ** -0.5)
            o = _softmax_pv(_nt_dot(q.astype(BF16), k), v)
            o_ref[0, :, cols] = o.astype(o_ref.dtype)

    @pl.when(j < ctx_block)
    def _():
        attend(kb_ref[...], vb_ref[...])

    @pl.when(j >= ctx_block)
    def _():
        attend(kb_ref[n_lat:, :], vb_ref[n_lat:, :])


def _gqa(z, tables, qn_g, kn_g, n_lat):
    n_batch, t, _ = z.shape
    tq = 256
    gw = GQA_GROUP * HEAD_DIM
    q0 = 2 * MIX_HALF // gw
    k0 = (2 * MIX_HALF + GQA_Q_HEADS * HEAD_DIM) // HEAD_DIM
    v0 = k0 + GQA_KV_HEADS
    cos, sin_up, sin_dn = tables
    tq_spec = pl.BlockSpec((tq, HEAD_DIM), lambda b, h, j: (j, 0))
    tk_spec = pl.BlockSpec((t, HEAD_DIM), lambda b, h, j: (0, 0))
    g_spec = pl.BlockSpec((1, HEAD_DIM), lambda b, h, j: (0, 0))
    return pl.pallas_call(
        functools.partial(_gqa_kernel, n_lat=n_lat, ctx_block=n_lat // tq),
        grid=(n_batch, GQA_KV_HEADS, t // tq),
        in_specs=[
            pl.BlockSpec((1, tq, gw), lambda b, h, j: (b, j, q0 + h)),
            pl.BlockSpec((1, t, HEAD_DIM), lambda b, h, j: (b, 0, k0 + h)),
            pl.BlockSpec((1, t, HEAD_DIM), lambda b, h, j: (b, 0, v0 + h)),
            tq_spec, tq_spec, tq_spec, tk_spec, tk_spec, tk_spec, g_spec, g_spec,
        ],
        out_specs=pl.BlockSpec((1, tq, gw), lambda b, h, j: (b, j, h)),
        out_shape=jax.ShapeDtypeStruct((n_batch, t, GQA_Q_HEADS * HEAD_DIM), BF16),
        scratch_shapes=[pltpu.VMEM((t, HEAD_DIM), BF16), pltpu.VMEM((t, HEAD_DIM), BF16)],
        compiler_params=_params(3),
        name="gqa_attention",
    )(z, z, z, cos, sin_up, sin_dn, cos, sin_up, sin_dn,
      qn_g.reshape(1, HEAD_DIM), kn_g.reshape(1, HEAD_DIM))


CONV_ROWS = 32
CONV_HALO = 32
CONV_GAP = 16


def _conformer_kernel(a_ref, g_ref, w_ref, b_ref, lng_ref, lnb_ref, o_ref, u_ref, *, n_lat, n_ctx):
    ch = a_ref.shape[2]
    n_lane_blocks = ch // HEAD_DIM
    lat0 = CONV_GAP
    ctx0 = 2 * CONV_GAP + n_lat
    zeros = jnp.zeros((CONV_GAP, ch), F32)
    u_ref[0:CONV_GAP, :] = zeros
    u_ref[lat0 + n_lat:ctx0, :] = zeros
    u_ref[ctx0 + n_ctx:ctx0 + n_ctx + CONV_GAP, :] = zeros

    glu_rows = 256

    def glu(i, carry):
        src = pl.multiple_of(i * glu_rows, glu_rows)
        dst = pl.multiple_of(jnp.where(src >= n_lat, src + 2 * CONV_GAP, src + CONV_GAP), 8)
        a = a_ref[0, pl.ds(src, glu_rows), :].astype(F32)
        g = g_ref[0, pl.ds(src, glu_rows), :].astype(F32)
        u_ref[pl.ds(dst, glu_rows), :] = a * jax.nn.sigmoid(g)
        return carry

    lax.fori_loop(0, (n_lat + n_ctx) // glu_rows, glu, 0)

    win_rows = CONV_ROWS + CONV_HALO

    def conv_rows(i, carry):
        out_row = pl.multiple_of(i * CONV_ROWS, CONV_ROWS)
        win0 = pl.multiple_of(jnp.where(out_row >= n_lat, out_row + CONV_GAP, out_row), 8)
        accs = []
        for c in range(n_lane_blocks):
            lanes = slice(c * HEAD_DIM, (c + 1) * HEAD_DIM)
            win = u_ref[pl.ds(win0, win_rows), lanes]
            shifted = [win] + [pltpu.roll(win, win_rows - r, 0) for r in range(1, 8)]
            acc = jnp.zeros((CONV_ROWS, HEAD_DIM), F32)
            for k in range(CONV_WIDTH):
                off = k + CONV_GAP - CONV_PAD
                q, r = divmod(off, 8)
                acc = acc + w_ref[k:k + 1, lanes] * shifted[r][8 * q:8 * q + CONV_ROWS, :]
            accs.append(acc + b_ref[:, lanes])
        total = accs[0]
        for acc in accs[1:]:
            total = total + acc
        mu = jnp.sum(total, axis=-1, keepdims=True) * (1.0 / ch)
        cen = [acc - mu for acc in accs]
        sq = cen[0] * cen[0]
        for cc in cen[1:]:
            sq = sq + cc * cc
        rstd = lax.rsqrt(jnp.sum(sq, axis=-1, keepdims=True) * (1.0 / ch) + LN_EPS)
        for c in range(n_lane_blocks):
            lanes = slice(c * HEAD_DIM, (c + 1) * HEAD_DIM)
            y = cen[c] * rstd * lng_ref[:, lanes] + lnb_ref[:, lanes]
            o_ref[0, pl.ds(out_row, CONV_ROWS), lanes] = _silu(y).astype(o_ref.dtype)
        return carry

    lax.fori_loop(0, (n_lat + n_ctx) // CONV_ROWS, conv_rows, 0)


def _conformer(z, dw_w, dw_b, ln_g, ln_b, n_lat):
    n_batch, t, _ = z.shape
    ch = MIX_HALF
    n_ctx = t - n_lat
    pad_rows = t + 3 * CONV_GAP
    row = lambda v: v.reshape(1, ch)
    vec_spec = pl.BlockSpec((1, ch), lambda b: (0, 0))
    return pl.pallas_call(
        functools.partial(_conformer_kernel, n_lat=n_lat, n_ctx=n_ctx),
        grid=(n_batch,),
        in_specs=[
            pl.BlockSpec((1, t, ch), lambda b: (b, 0, 0)),
            pl.BlockSpec((1, t, ch), lambda b: (b, 0, 1)),
            pl.BlockSpec((CONV_WIDTH, ch), lambda b: (0, 0)),
            vec_spec, vec_spec, vec_spec,
        ],
        out_specs=pl.BlockSpec((1, t, ch), lambda b: (b, 0, 0)),
        out_shape=jax.ShapeDtypeStruct((n_batch, t, ch), BF16),
        scratch_shapes=[pltpu.VMEM((pad_rows, ch), F32)],
        compiler_params=_params(1),
        name="conformer_conv",
    )(z, z, dw_w, row(dw_b), row(ln_g), row(ln_b))


def _dft_matrix(n_lat):
    m = 2 * n_lat
    n_tiles = n_lat // FREQ_TILE
    r = lax.broadcasted_iota(jnp.int32, (m, n_lat), 0)
    t = lax.broadcasted_iota(jnp.int32, (m, n_lat), 1)
    tile = r // (2 * FREQ_TILE)
    within = r % (2 * FREQ_TILE)
    is_sin = within >= FREQ_TILE
    f = tile * FREQ_TILE + within % FREQ_TILE
    ang = ((f * t) % m).astype(F32) * (2.0 * math.pi / m)
    nyq = jnp.where(t % 2 == 0, 1.0, -1.0)
    w = jnp.where(is_sin, jnp.where(f == 0, nyq, jnp.sin(ang)), jnp.cos(ang))
    del n_tiles
    return w.astype(BF16), w.T.astype(BF16)


def _filter_kernel(feat_ref, w1_ref, b1_ref, w2_ref, b2_ref, w3_ref, b3_ref, fr_ref, w4_ref,
                   t_ref, dl_ref, o_ref, *, n_tiles):
    fr = fr_ref[...]
    h = jnp.sin(fr[0:1, :] * (_dot_hi(feat_ref[...], w1_ref[...]) + b1_ref[...]))
    h = jnp.sin(fr[1:2, :] * (_dot_hi(h, w2_ref[...]) + b2_ref[...]))
    h = jnp.sin(fr[2:3, :] * (_dot_hi(h, w3_ref[...]) + b3_ref[...]))
    hh = _dot_hi(h, w4_ref[...]) * jnp.exp(-t_ref[...] * dl_ref[...])
    hh = hh / (jnp.sum(jnp.abs(hh), axis=0, keepdims=True) + 1e-6)
    is_bwd = pl.program_id(0) >= n_tiles // 2
    row = lax.broadcasted_iota(jnp.int32, hh.shape, 0)
    o_ref[...] = jnp.where(jnp.logical_and(is_bwd, row == 0), 0.0, hh).astype(o_ref.dtype)


def _hyena_filters(n_lat, w1, b1, w2, b2, w3, b3, w4, freq):
    fw = w1.shape[1]
    n = w4.shape[1]
    tn = 512
    bands = (HYENA_POS_DIM - 1) // 2
    t = np.linspace(0.0, 1.0, n_lat)[:, None]
    w = (2.0 * math.pi / n_lat) * np.arange(n_lat)[:, None]
    f = np.linspace(1e-4, bands - 1, bands)[None, :]
    feat = np.concatenate([t, np.cos(f * w), -np.sin(f * w),
                           np.zeros((n_lat, HYENA_POS_PAD - HYENA_POS_DIM))], axis=-1)
    deltas = np.abs(np.linspace(HYENA_MIN_DECAY, HYENA_MAX_DECAY, HYENA_CH))
    deltas = np.tile(deltas, n // HYENA_CH)[None, :]
    w1p = jnp.concatenate([w1, jnp.zeros((HYENA_POS_PAD - HYENA_POS_DIM, fw), F32)], axis=0)
    full = lambda shape: pl.BlockSpec(shape, lambda j: (0, 0))
    return pl.pallas_call(
        functools.partial(_filter_kernel, n_tiles=n // tn),
        grid=(n // tn,),
        in_specs=[
            full((n_lat, HYENA_POS_PAD)), full((HYENA_POS_PAD, fw)), full((1, fw)),
            full((fw, fw)), full((1, fw)), full((fw, fw)), full((1, fw)), full((3, fw)),
            pl.BlockSpec((fw, tn), lambda j: (0, j)),
            full((n_lat, 1)),
            pl.BlockSpec((1, tn), lambda j: (0, j)),
        ],
        out_specs=pl.BlockSpec((n_lat, tn), lambda j: (0, j)),
        out_shape=jax.ShapeDtypeStruct((n_lat, n), BF16),
        compiler_params=_params(1),
        name="hyena_filters",
    )(jnp.asarray(feat, F32), w1p, b1.reshape(1, fw), w2, b2.reshape(1, fw), w3, b3.reshape(1, fw),
      freq, w4, jnp.asarray(t, F32), jnp.asarray(deltas, F32))


def _filter_dft_kernel(wf_ref, hf_ref, hb_ref, ka_ref, kb_ref, k2_ref, *, scale):
    gf = _dot(wf_ref[...], hf_ref[...])
    gb = _dot(wf_ref[...], hb_ref[...])
    kr = (gf[:FREQ_TILE] + gb[:FREQ_TILE]) * scale
    ki = (gb[FREQ_TILE:] - gf[FREQ_TILE:]) * scale
    nyq = (gf[FREQ_TILE:FREQ_TILE + 1] + gb[FREQ_TILE:FREQ_TILE + 1]) * scale
    row = lax.broadcasted_iota(jnp.int32, kr.shape, 0)
    first = jnp.logical_and(row == 0, pl.program_id(0) == 0)
    ka_ref[0] = jnp.where(first, 0.5 * kr, kr)
    kb_ref[0] = jnp.where(first, 0.5 * nyq, kr)
    k2_ref[0] = jnp.where(first, 0.0, ki)


def _filter_dft(wf, hmat, n_lat):
    tn = 512
    n_ct = HYENA_CH // tn
    out = jax.ShapeDtypeStruct((HYENA_ORDER, n_lat, HYENA_CH), F32)
    o_spec = pl.BlockSpec((1, FREQ_TILE, tn), lambda i, n, c: (n, i, c))
    return pl.pallas_call(
        functools.partial(_filter_dft_kernel, scale=1.0 / n_lat),
        grid=(n_lat // FREQ_TILE, HYENA_ORDER, n_ct),
        in_specs=[
            pl.BlockSpec((2 * FREQ_TILE, n_lat), lambda i, n, c: (i, 0)),
            pl.BlockSpec((n_lat, tn), lambda i, n, c: (0, n * n_ct + c)),
            pl.BlockSpec((n_lat, tn), lambda i, n, c: (0, (HYENA_ORDER + n) * n_ct + c)),
        ],
        out_specs=[o_spec, o_spec, o_spec],
        out_shape=[out, out, out],
        compiler_params=_params(3),
        name="hyena_filter_dft",
    )(wf, hmat, hmat)


def _short_conv_kernel(z_ref, w_ref, b_ref, o_ref):
    z = z_ref[0].astype(F32)
    n = z.shape[0]
    row = lax.broadcasted_iota(jnp.int32, z.shape, 0)
    prev = jnp.where(row == 0, 0.0, pltpu.roll(z, 1, 0))
    nxt = jnp.where(row == n - 1, 0.0, pltpu.roll(z, n - 1, 0))
    o = w_ref[0:1, :] * prev + w_ref[1:2, :] * z + w_ref[2:3, :] * nxt + b_ref[...]
    o_ref[0] = o.astype(o_ref.dtype)


def _short_conv(z, w, b, n_lat):
    n_batch = z.shape[0]
    n = w.shape[1]
    tc = 512
    return pl.pallas_call(
        _short_conv_kernel,
        grid=(n_batch, n // tc),
        in_specs=[pl.BlockSpec((1, n_lat, tc), lambda bb, c: (bb, 0, c)),
                  pl.BlockSpec((3, tc), lambda bb, c: (0, c)),
                  pl.BlockSpec((1, tc), lambda bb, c: (0, c))],
        out_specs=pl.BlockSpec((1, n_lat, tc), lambda bb, c: (bb, 0, c)),
        out_shape=jax.ShapeDtypeStruct((n_batch, n_lat, n), BF16),
        compiler_params=_params(2),
        name="hyena_short_conv",
    )(z, w, b.reshape(1, n))


def _spectrum_kernel(wf_ref, u_ref, ka_ref, kb_ref, k2_ref, o_ref):
    g = _dot(wf_ref[...], u_ref[0])
    p, q = g[:FREQ_TILE], g[FREQ_TILE:]
    o_ref[0, :FREQ_TILE, :] = (p * ka_ref[0] + q * k2_ref[0]).astype(o_ref.dtype)
    o_ref[0, FREQ_TILE:, :] = (q * kb_ref[0] - p * k2_ref[0]).astype(o_ref.dtype)


def _spectrum(wf, u, u_col0, spectra, order, n_lat):
    n_batch = u.shape[0]
    tn = 512
    n_ct = HYENA_CH // tn
    c0 = u_col0 // tn
    k_spec = pl.BlockSpec((1, FREQ_TILE, tn), lambda i, c, b: (order, i, c))
    return pl.pallas_call(
        _spectrum_kernel,
        grid=(n_lat // FREQ_TILE, n_ct, n_batch),
        in_specs=[
            pl.BlockSpec((2 * FREQ_TILE, n_lat), lambda i, c, b: (i, 0)),
            pl.BlockSpec((1, n_lat, tn), lambda i, c, b: (b, 0, c0 + c)),
            k_spec, k_spec, k_spec,
        ],
        out_specs=pl.BlockSpec((1, 2 * FREQ_TILE, tn), lambda i, c, b: (b, i, c)),
        out_shape=jax.ShapeDtypeStruct((n_batch, 2 * n_lat, HYENA_CH), BF16),
        compiler_params=_params(3),
        name="hyena_spectrum",
    )(wf, u, *spectra)


def _inverse_gate_kernel(wt_ref, h_ref, u_ref, x_ref, skip_ref, o_ref):
    y = _dot(wt_ref[...], h_ref[0])
    u = u_ref[0].astype(F32)
    o_ref[0] = (x_ref[0].astype(F32) * (y + skip_ref[0] * u)).astype(o_ref.dtype)


def _inverse_gate(wft, h, u, u_col0, x, x_col0, skip, order, n_lat):
    n_batch = h.shape[0]
    tn = 512
    tt = _tile(n_lat, 512)
    n_ct = HYENA_CH // tn
    uc0 = u_col0 // tn
    xc0 = x_col0 // tn
    return pl.pallas_call(
        _inverse_gate_kernel,
        grid=(n_lat // tt, n_ct, n_batch),
        in_specs=[
            pl.BlockSpec((tt, 2 * n_lat), lambda i, c, b: (i, 0)),
            pl.BlockSpec((1, 2 * n_lat, tn), lambda i, c, b: (b, 0, c)),
            pl.BlockSpec((1, tt, tn), lambda i, c, b: (b, i, uc0 + c)),
            pl.BlockSpec((1, tt, tn), lambda i, c, b: (b, i, xc0 + c)),
            pl.BlockSpec((1, 1, tn), lambda i, c, b: (order, 0, c)),
        ],
        out_specs=pl.BlockSpec((1, tt, tn), lambda i, c, b: (b, i, c)),
        out_shape=jax.ShapeDtypeStruct((n_batch, n_lat, HYENA_CH), BF16),
        compiler_params=_params(3),
        name="hyena_inverse_gate",
    )(wft, h, u, x, skip.reshape(HYENA_ORDER, 1, HYENA_CH))


def _hyena(z, short_w, short_b, filt, skip, n_lat):
    wf, wft = _dft_matrix(n_lat)
    hmat = _hyena_filters(n_lat, *filt)
    spectra = _filter_dft(wf, hmat, n_lat)
    zc = _short_conv(z, short_w, short_b, n_lat)
    h0 = _spectrum(wf, zc, 2 * HYENA_CH, spectra, 0, n_lat)
    y1 = _inverse_gate(wft, h0, zc, 2 * HYENA_CH, zc, 0, skip, 0, n_lat)
    h1 = _spectrum(wf, y1, 0, spectra, 1, n_lat)
    return _inverse_gate(wft, h1, y1, 0, zc, HYENA_CH, skip, 1, n_lat)


def _norm_mm_kernel(a_ref, g_ref, w_ref, o_ref):
    a = a_ref[0].astype(F32)
    a = a * lax.rsqrt(jnp.mean(a * a, axis=-1, keepdims=True) + NORM_EPS) * g_ref[...]
    o_ref[0] = _dot(a.astype(BF16), w_ref[...]).astype(o_ref.dtype)


def _norm_mm(z, col0, g, w, rows, tm, name):
    n_batch = z.shape[0]
    k, n = w.shape
    c0 = col0 // k
    tm = _tile(rows, tm)
    return pl.pallas_call(
        _norm_mm_kernel,
        grid=(n_batch, rows // tm),
        in_specs=[pl.BlockSpec((1, tm, k), lambda b, i: (b, i, c0)),
                  pl.BlockSpec((1, k), lambda b, i: (0, 0)),
                  pl.BlockSpec((k, n), lambda b, i: (0, 0))],
        out_specs=pl.BlockSpec((1, tm, n), lambda b, i: (b, i, 0)),
        out_shape=jax.ShapeDtypeStruct((n_batch, rows, n), BF16),
        compiler_params=_params(2),
        name=name,
    )(z, g.reshape(1, k), w)


def _mla_kernel(q_ref, kn_ref, v_ref, kr_ref, cq_ref, uq_ref, dq_ref, ck_ref, uk_ref, dk_ref,
                o_ref, kb_ref):
    j = pl.program_id(2)
    half = MLA_ROPE // 4

    @pl.when(j == 0)
    def _():
        kb_ref[:, :HEAD_DIM] = kn_ref[0]
        kr = kr_ref[0].astype(F32)
        kb_ref[:, HEAD_DIM:] = _rope128(kr, ck_ref[...], uk_ref[...], dk_ref[...], half).astype(BF16)

    scale = MLA_QK ** -0.5
    qn = (q_ref[0, :, :HEAD_DIM].astype(F32) * scale).astype(BF16)
    qr = q_ref[0, :, HEAD_DIM:].astype(F32)
    qr = (_rope128(qr, cq_ref[...], uq_ref[...], dq_ref[...], half) * scale).astype(BF16)
    s = _nt_dot(jnp.concatenate([qn, qr], axis=1), kb_ref[...])
    o_ref[0] = _softmax_pv(s, v_ref[0]).astype(o_ref.dtype)


def _mla(q, kv, z, tables, n_lat):
    n_batch, t, _ = kv.shape
    tq = _tile(n_lat, 512)
    kr_block = (IN_ODD - MLA_ROPE) // HEAD_DIM
    cos, sin_up, sin_dn = tables
    tq_spec = pl.BlockSpec((tq, HEAD_DIM), lambda b, h, j: (j, 0))
    tk_spec = pl.BlockSpec((t, HEAD_DIM), lambda b, h, j: (0, 0))
    return pl.pallas_call(
        _mla_kernel,
        grid=(n_batch, MLA_HEADS, n_lat // tq),
        in_specs=[
            pl.BlockSpec((1, tq, MLA_QK_PAD), lambda b, h, j: (b, j, h)),
            pl.BlockSpec((1, t, HEAD_DIM), lambda b, h, j: (b, 0, 2 * h)),
            pl.BlockSpec((1, t, HEAD_DIM), lambda b, h, j: (b, 0, 2 * h + 1)),
            pl.BlockSpec((1, t, HEAD_DIM), lambda b, h, j: (b, 0, kr_block)),
            tq_spec, tq_spec, tq_spec, tk_spec, tk_spec, tk_spec,
        ],
        out_specs=pl.BlockSpec((1, tq, HEAD_DIM), lambda b, h, j: (b, j, h)),
        out_shape=jax.ShapeDtypeStruct((n_batch, n_lat, MLA_HEADS * HEAD_DIM), BF16),
        scratch_shapes=[pltpu.VMEM((t, MLA_QK_PAD), BF16)],
        compiler_params=_params(3),
        name="mla_attention",
    )(q, kv, kv, z, cos, sin_up, sin_dn, cos, sin_up, sin_dn)


def _pad_uq(w_uq):
    rank = w_uq.shape[0]
    w = w_uq.reshape(rank, MLA_HEADS, MLA_QK)
    w = jnp.concatenate([w, jnp.zeros((rank, MLA_HEADS, MLA_QK_PAD - MLA_QK), w.dtype)], axis=-1)
    return w.reshape(rank, MLA_HEADS * MLA_QK_PAD)


def _ffn(s, rows, norm_g, mod, wg, wu, wd, tm_resid, ctx_rows_from):
    n_batch = s.shape[0]
    h = _norm_mod(s, norm_g, mod, 3, 4, rows, ctx_rows_from)
    act = _mm_glu(h.reshape(n_batch * rows, D_MODEL), wg, wu, 1024, 512)
    act = act.reshape(n_batch, rows, wg.shape[1])
    return _mm_resid([act], [wd], s, mod, 5, rows, tm_resid, 512, ctx_rows_from, "ffn_down")


def kernel(x, c, ctx, c_ctx, ada_w, ada_b, norm_mix_g, norm_ffn_g, e_w_in, e_w_out, e_dw_w, e_dw_b, e_ln_g, e_ln_b, e_qn_g, e_kn_g, o_w_in, o_w_out, o_short_w, o_short_b, o_f_w1, o_f_b1, o_f_w2, o_f_b2, o_f_w3, o_f_b3, o_f_w4, o_f_freq, o_skip, o_q_norm_g, o_kv_norm_g, o_w_uq, o_w_ukv, ffn_w_gate, ffn_w_up, ffn_w_down, final_norm_g):
    n_batch, n_lat, d = x.shape
    n_ctx = ctx.shape[1]
    t = n_lat + n_ctx
    bf = lambda w: w.astype(BF16)

    n_rows = -(-(n_batch + 1) // 8) * 8
    c_rows = jnp.concatenate(
        [c, c_ctx[None, :], jnp.zeros((n_rows - n_batch - 1, d), F32)], axis=0)
    mod = _ada(c_rows, ada_w, ada_b).reshape(ada_w.shape[0], n_rows * 6, 1, d)

    s = jnp.concatenate([x, ctx], axis=1)

    h = _norm_mod(s, norm_mix_g[0], mod[0], 0, 1, t, n_lat)
    z = _mm(h.reshape(n_batch * t, d), bf(e_w_in[0]), 1024, 512, "even_w_in").reshape(n_batch, t, -1)
    conv = _conformer(z, e_dw_w[0], e_dw_b[0], e_ln_g[0], e_ln_b[0], n_lat)
    att = _gqa(z, _axial_tables(n_lat, n_ctx, HEAD_DIM // 2), e_qn_g[0], e_kn_g[0], n_lat)
    w_out = bf(e_w_out[0])
    s = _mm_resid([conv, att], [w_out[:MIX_HALF], w_out[MIX_HALF:]], s, mod[0], 2, t, 768, 1024,
                  n_lat, "even_w_out")
    s = _ffn(s, t, norm_ffn_g[0], mod[0], bf(ffn_w_gate[0]), bf(ffn_w_up[0]), bf(ffn_w_down[0]),
             768, n_lat)

    h = _norm_mod(s, norm_mix_g[1], mod[1], 0, 1, t, n_lat)
    w_in = jnp.concatenate([o_w_in[0], jnp.zeros((d, IN_ODD_PAD - IN_ODD), F32)], axis=1)
    z = _mm(h.reshape(n_batch * t, d), bf(w_in), 1024, 512, "odd_w_in").reshape(n_batch, t, -1)
    filt = (o_f_w1[0], o_f_b1[0], o_f_w2[0], o_f_b2[0], o_f_w3[0], o_f_b3[0], o_f_w4[0], o_f_freq[0])
    y_h = _hyena(z, o_short_w[0], o_short_b[0], filt, o_skip[0], n_lat)
    q = _norm_mm(z, HYENA_IN, o_q_norm_g[0], bf(_pad_uq(o_w_uq[0])), n_lat, 1024, "mla_q_up")
    kv = _norm_mm(z, HYENA_IN + MLA_Q_RANK, o_kv_norm_g[0], bf(o_w_ukv[0]), t, 768, "mla_kv_up")
    att = _mla(q, kv, z, _axial_tables(n_lat, n_ctx, MLA_ROPE // 2), n_lat)
    w_out = bf(o_w_out[0])
    xs = _mm_resid([y_h, att], [w_out[:MIX_HALF], w_out[MIX_HALF:]], s, mod[1], 2, n_lat, 1024, 1024,
                   None, "odd_w_out")
    xs = _ffn(xs, n_lat, norm_ffn_g[1], mod[1], bf(ffn_w_gate[1]), bf(ffn_w_up[1]), bf(ffn_w_down[1]),
              512, None)
    return _final_norm(xs, final_norm_g)
```

```python
import functools
import math

import numpy as np
import jax
import jax.numpy as jnp
from jax import lax
from jax.experimental import pallas as pl
from jax.experimental.pallas import tpu as pltpu

F32 = jnp.float32
BF16 = jnp.bfloat16

D_MODEL = 2048
GRID_W = 64
HEAD_DIM = 128
ROPE_THETA = 10000.0
NORM_EPS = 1e-6
LN_EPS = 1e-5
MIX_HALF = D_MODEL // 2
CONV_WIDTH = 31
CONV_PAD = (CONV_WIDTH - 1) // 2
GQA_Q_HEADS = MIX_HALF // HEAD_DIM
GQA_KV_HEADS = GQA_Q_HEADS // 4
GQA_GROUP = GQA_Q_HEADS // GQA_KV_HEADS
HYENA_CH = MIX_HALF
HYENA_ORDER = 2
HYENA_POS_DIM = 33
HYENA_POS_PAD = 40
HYENA_FAST_DECAY = 0.3
HYENA_SLOW_DECAY = 1.5
HYENA_TARGET = 1e-2
HYENA_MAX_DECAY = math.log(HYENA_TARGET) / HYENA_FAST_DECAY
HYENA_MIN_DECAY = math.log(HYENA_TARGET) / HYENA_SLOW_DECAY
HYENA_IN = 3 * HYENA_CH
MLA_HEADS = MIX_HALF // HEAD_DIM
MLA_Q_RANK = D_MODEL // 4
MLA_KV_RANK = D_MODEL // 8
MLA_NOPE = HEAD_DIM
MLA_ROPE = HEAD_DIM // 2
MLA_QK = MLA_NOPE + MLA_ROPE
MLA_QK_PAD = 2 * HEAD_DIM
IN_ODD = HYENA_IN + MLA_Q_RANK + MLA_KV_RANK + MLA_ROPE
IN_ODD_PAD = 4096

V7X_VMEM_LIMIT_BYTES = 56 * 1024 * 1024
FREQ_TILE = 512


def _params(n_axes):
    return pltpu.CompilerParams(
        dimension_semantics=("arbitrary",) * n_axes,
        vmem_limit_bytes=V7X_VMEM_LIMIT_BYTES,
    )


ROW_ALIGN = 256


def _tile(n, target, align=ROW_ALIGN):
    best = None
    for cand in range(align, min(n, target) + 1, align):
        if n % cand == 0:
            best = cand
    assert best is not None, (n, target, align)
    return best


def _dot(a, b):
    return jnp.dot(a, b, preferred_element_type=F32)


def _dot_hi(a, b):
    return jnp.dot(a, b, preferred_element_type=F32, precision=lax.Precision.HIGHEST)


def _silu(v):
    return v * jax.nn.sigmoid(v)


def _ada_kernel(c_ref, w_ref, b_ref, o_ref):
    o_ref[0] = _dot_hi(_silu(c_ref[...]), w_ref[0]) + b_ref[0]


def _ada(c_rows, ada_w, ada_b):
    depth, d, n = ada_w.shape
    r = c_rows.shape[0]
    tn = 512
    return pl.pallas_call(
        _ada_kernel,
        grid=(depth, n // tn),
        in_specs=[
            pl.BlockSpec((r, d), lambda l, j: (0, 0)),
            pl.BlockSpec((1, d, tn), lambda l, j: (l, 0, j)),
            pl.BlockSpec((1, 1, tn), lambda l, j: (l, 0, j)),
        ],
        out_specs=pl.BlockSpec((1, r, tn), lambda l, j: (l, 0, j)),
        out_shape=jax.ShapeDtypeStruct((depth, r, n), F32),
        compiler_params=_params(2),
        name="ada_mod",
    )(c_rows, ada_w, ada_b.reshape(depth, 1, n))


def _norm_matmul_kernel(x_ref, g_ref, shb_ref, scb_ref, shc_ref, scc_ref, *rest, ctx_sub, glu):
    if glu:
        wg_ref, wu_ref, o_ref, h_ref = rest
    else:
        w_ref, o_ref, h_ref = rest

    @pl.when(pl.program_id(2) == 0)
    def _():
        i = pl.program_id(1)
        n_sub = h_ref.shape[0] // ROW_ALIGN
        for s in range(n_sub):
            rows = slice(s * ROW_ALIGN, (s + 1) * ROW_ALIGN)
            x = x_ref[0, rows, :]
            y = x * lax.rsqrt(jnp.mean(x * x, axis=-1, keepdims=True) + NORM_EPS) * g_ref[...]
            shift, scale = shb_ref[0], scb_ref[0]
            if ctx_sub is not None:
                is_ctx = i * n_sub + s >= ctx_sub
                shift = jnp.where(is_ctx, shc_ref[0], shift)
                scale = jnp.where(is_ctx, scc_ref[0], scale)
            h_ref[rows, :] = (y * (1.0 + scale) + shift).astype(h_ref.dtype)

    h = h_ref[...]
    if glu:
        o_ref[0] = (_silu(_dot(h, wg_ref[...])) * _dot(h, wu_ref[...])).astype(o_ref.dtype)
    else:
        o_ref[0] = _dot(h, w_ref[...]).astype(o_ref.dtype)


def _norm_matmul(s, rows, g, mod, k_shift, k_scale, weights, tm, tn, ctx_rows_from, name):
    n_batch, _, d = s.shape
    n = weights[0].shape[1]
    tm = _tile(rows, tm)
    ctx_sub = None if ctx_rows_from is None else ctx_rows_from // ROW_ALIGN
    x_spec = pl.BlockSpec((1, tm, d), lambda b, i, j: (b, i, 0))
    batch_mod = lambda k: pl.BlockSpec((1, 1, d), lambda b, i, j: (b * 6 + k, 0, 0))
    ctx_mod = lambda k: pl.BlockSpec((1, 1, d), lambda b, i, j: (n_batch * 6 + k, 0, 0))
    w_spec = pl.BlockSpec((d, tn), lambda b, i, j: (0, j))
    return pl.pallas_call(
        functools.partial(_norm_matmul_kernel, ctx_sub=ctx_sub, glu=len(weights) == 2),
        grid=(n_batch, rows // tm, n // tn),
        in_specs=[x_spec, pl.BlockSpec((1, d), lambda b, i, j: (0, 0)),
                  batch_mod(k_shift), batch_mod(k_scale), ctx_mod(k_shift), ctx_mod(k_scale)]
                 + [w_spec] * len(weights),
        out_specs=pl.BlockSpec((1, tm, tn), lambda b, i, j: (b, i, j)),
        out_shape=jax.ShapeDtypeStruct((n_batch, rows, n), BF16),
        scratch_shapes=[pltpu.VMEM((tm, d), BF16)],
        compiler_params=_params(3),
        name=name,
    )(s, g.reshape(1, d), mod, mod, mod, mod, *weights)


def _final_norm_kernel(x_ref, g_ref, o_ref):
    x = x_ref[0]
    o_ref[0] = x * lax.rsqrt(jnp.mean(x * x, axis=-1, keepdims=True) + NORM_EPS) * g_ref[...]


def _final_norm(s, g):
    n_batch, rows, d = s.shape
    tr = 256
    return pl.pallas_call(
        _final_norm_kernel,
        grid=(n_batch, rows // tr),
        in_specs=[pl.BlockSpec((1, tr, d), lambda b, j: (b, j, 0)),
                  pl.BlockSpec((1, d), lambda b, j: (0, 0))],
        out_specs=pl.BlockSpec((1, tr, d), lambda b, j: (b, j, 0)),
        out_shape=jax.ShapeDtypeStruct((n_batch, rows, d), F32),
        compiler_params=_params(2),
        name="final_norm",
    )(s, g.reshape(1, d))


def _mm_resid_kernel(*refs, n_a, ctx_sub):
    a_refs = refs[:n_a]
    w_refs = refs[n_a:2 * n_a]
    r_ref, gb_ref, gc_ref, o_ref = refs[2 * n_a:]
    acc = _dot(a_refs[0][0], w_refs[0][...])
    for a_ref, w_ref in zip(a_refs[1:], w_refs[1:]):
        acc = acc + _dot(a_ref[0], w_ref[...])
    i = pl.program_id(2)
    n_sub = 1 if ctx_sub is None else acc.shape[0] // ROW_ALIGN
    sub = acc.shape[0] // n_sub
    for s in range(n_sub):
        gate = gb_ref[0]
        if ctx_sub is not None:
            gate = jnp.where(i * n_sub + s >= ctx_sub, gc_ref[0], gate)
        rows = slice(s * sub, (s + 1) * sub)
        o_ref[0, rows, :] = r_ref[0, rows, :] + gate * acc[rows, :]


def _mm_resid(a_list, w_list, resid, mod, k_gate, rows, tm, tn, ctx_rows_from, name):
    n_batch = resid.shape[0]
    n = w_list[0].shape[1]
    tm = _tile(rows, tm)
    ctx_sub = None if ctx_rows_from is None else ctx_rows_from // ROW_ALIGN
    ctx_row = n_batch
    in_specs = [pl.BlockSpec((1, tm, a.shape[2]), lambda j, b, i: (b, i, 0)) for a in a_list]
    in_specs += [pl.BlockSpec((w.shape[0], tn), lambda j, b, i: (0, j)) for w in w_list]
    in_specs += [
        pl.BlockSpec((1, tm, tn), lambda j, b, i: (b, i, j)),
        pl.BlockSpec((1, 1, tn), lambda j, b, i: (b * 6 + k_gate, 0, j)),
        pl.BlockSpec((1, 1, tn), lambda j, b, i: (ctx_row * 6 + k_gate, 0, j)),
    ]
    return pl.pallas_call(
        functools.partial(_mm_resid_kernel, n_a=len(a_list), ctx_sub=ctx_sub),
        grid=(n // tn, n_batch, rows // tm),
        in_specs=in_specs,
        out_specs=pl.BlockSpec((1, tm, tn), lambda j, b, i: (b, i, j)),
        out_shape=jax.ShapeDtypeStruct((n_batch, rows, n), F32),
        compiler_params=_params(3),
        name=name,
    )(*a_list, *w_list, resid, mod, mod)


def _axial_tables(n_lat, n_ctx, axis_dims):
    half = axis_dims // 2
    t = np.arange(n_lat)
    pos = np.stack([t // GRID_W, t % GRID_W], axis=1).astype(np.float64)
    inv = ROPE_THETA ** (-np.arange(half, dtype=np.float64) / half)
    lane = np.arange(HEAD_DIM)
    active = lane < 2 * axis_dims
    axis = np.where(active, lane // axis_dims, 0)
    within = lane % axis_dims
    ang = pos[:, axis] * inv[within % half][None, :]
    cos = np.where(active[None, :], np.cos(ang), 1.0)
    sin = np.where(active[None, :], np.sin(ang), 0.0)
    first = (within < half)[None, :]
    sin_up = np.where(first, -sin, 0.0)
    sin_dn = np.where(first, 0.0, sin)
    ident = np.zeros((n_ctx, HEAD_DIM))
    cos = np.concatenate([cos, ident + 1.0], axis=0)
    sin_up = np.concatenate([sin_up, ident], axis=0)
    sin_dn = np.concatenate([sin_dn, ident], axis=0)
    return tuple(jnp.asarray(v, dtype=F32) for v in (cos, sin_up, sin_dn))


def _rope128(v, cos, sin_up, sin_dn, half):
    return (v * cos + pltpu.roll(v, HEAD_DIM - half, 1) * sin_up
            + pltpu.roll(v, half, 1) * sin_dn)


LOG2E = 1.4426950408889634


def _softmax_pv(s, v_ones):
    p = jnp.exp2(s - jnp.max(s, axis=-1, keepdims=True))
    o = _dot(p.astype(BF16), v_ones)
    return o[:, :HEAD_DIM] / o[:, HEAD_DIM:HEAD_DIM + 1]


def _store_v_ones(vb_ref, v):
    vb_ref[:, :HEAD_DIM] = v
    vb_ref[:, HEAD_DIM:] = jnp.ones(v.shape, v.dtype)


def _nt_dot(q, k):
    return lax.dot_general(q, k, (((1,), (1,)), ((), ())), preferred_element_type=F32)


def _gqa_kernel(q_ref, k_ref, v_ref, cq_ref, uq_ref, dq_ref, ck_ref, uk_ref, dk_ref,
                qg_ref, kg_ref, o_ref, kb_ref, vb_ref, *, n_lat, ctx_block):
    j = pl.program_id(2)
    half = HEAD_DIM // 4

    @pl.when(j == 0)
    def _():
        k = k_ref[0].astype(F32)
        k = k * lax.rsqrt(jnp.mean(k * k, axis=-1, keepdims=True) + NORM_EPS) * kg_ref[...]
        kb_ref[...] = _rope128(k, ck_ref[...], uk_ref[...], dk_ref[...], half).astype(BF16)
        _store_v_ones(vb_ref, v_ref[0])

    def attend(k, v):
        for g in range(GQA_GROUP):
            cols = slice(g * HEAD_DIM, (g + 1) * HEAD_DIM)
            q = q_ref[0, :, cols].astype(F32)
            q = q * lax.rsqrt(jnp.mean(q * q, axis=-1, keepdims=True) + NORM_EPS) * qg_ref[...]
            q = _rope128(q, cq_ref[...], uq_ref[...], dq_ref[...], half) * (HEAD_DIM ** -0.5 * LOG2E)
            o = _softmax_pv(_nt_dot(q.astype(BF16), k), v)
            o_ref[0, :, cols] = o.astype(o_ref.dtype)

    @pl.when(j < ctx_block)
    def _():
        attend(kb_ref[...], vb_ref[...])

    @pl.when(j >= ctx_block)
    def _():
        attend(kb_ref[n_lat:, :], vb_ref[n_lat:, :])


def _gqa(z, tables, qn_g, kn_g, n_lat):
    n_batch, t, _ = z.shape
    tq = 256
    gw = GQA_GROUP * HEAD_DIM
    q0 = 2 * MIX_HALF // gw
    k0 = (2 * MIX_HALF + GQA_Q_HEADS * HEAD_DIM) // HEAD_DIM
    v0 = k0 + GQA_KV_HEADS
    cos, sin_up, sin_dn = tables
    tq_spec = pl.BlockSpec((tq, HEAD_DIM), lambda b, h, j: (j, 0))
    tk_spec = pl.BlockSpec((t, HEAD_DIM), lambda b, h, j: (0, 0))
    g_spec = pl.BlockSpec((1, HEAD_DIM), lambda b, h, j: (0, 0))
    return pl.pallas_call(
        functools.partial(_gqa_kernel, n_lat=n_lat, ctx_block=n_lat // tq),
        grid=(n_batch, GQA_KV_HEADS, t // tq),
        in_specs=[
            pl.BlockSpec((1, tq, gw), lambda b, h, j: (b, j, q0 + h)),
            pl.BlockSpec((1, t, HEAD_DIM), lambda b, h, j: (b, 0, k0 + h)),
            pl.BlockSpec((1, t, HEAD_DIM), lambda b, h, j: (b, 0, v0 + h)),
            tq_spec, tq_spec, tq_spec, tk_spec, tk_spec, tk_spec, g_spec, g_spec,
        ],
        out_specs=pl.BlockSpec((1, tq, gw), lambda b, h, j: (b, j, h)),
        out_shape=jax.ShapeDtypeStruct((n_batch, t, GQA_Q_HEADS * HEAD_DIM), BF16),
        scratch_shapes=[pltpu.VMEM((t, HEAD_DIM), BF16), pltpu.VMEM((t, 2 * HEAD_DIM), BF16)],
        compiler_params=_params(3),
        name="gqa_attention",
    )(z, z, z, cos, sin_up, sin_dn, cos, sin_up, sin_dn,
      qn_g.reshape(1, HEAD_DIM), kn_g.reshape(1, HEAD_DIM))


CONV_ROWS = 32
CONV_HALO = 32
CONV_GAP = 16


def _conformer_kernel(a_ref, g_ref, w_ref, b_ref, lng_ref, lnb_ref, o_ref, u_ref, *, n_lat, n_ctx):
    ch = a_ref.shape[2]
    n_lane_blocks = ch // HEAD_DIM
    lat0 = CONV_GAP
    ctx0 = 2 * CONV_GAP + n_lat
    zeros = jnp.zeros((CONV_GAP, ch), F32)
    u_ref[0:CONV_GAP, :] = zeros
    u_ref[lat0 + n_lat:ctx0, :] = zeros
    u_ref[ctx0 + n_ctx:ctx0 + n_ctx + CONV_GAP, :] = zeros

    glu_rows = 256

    def glu(i, carry):
        src = pl.multiple_of(i * glu_rows, glu_rows)
        dst = pl.multiple_of(jnp.where(src >= n_lat, src + 2 * CONV_GAP, src + CONV_GAP), 8)
        a = a_ref[0, pl.ds(src, glu_rows), :].astype(F32)
        g = g_ref[0, pl.ds(src, glu_rows), :].astype(F32)
        u_ref[pl.ds(dst, glu_rows), :] = a * jax.nn.sigmoid(g)
        return carry

    lax.fori_loop(0, (n_lat + n_ctx) // glu_rows, glu, 0)

    win_rows = CONV_ROWS + CONV_HALO

    def conv_rows(i, carry):
        out_row = pl.multiple_of(i * CONV_ROWS, CONV_ROWS)
        win0 = pl.multiple_of(jnp.where(out_row >= n_lat, out_row + CONV_GAP, out_row), 8)
        accs = []
        for c in range(n_lane_blocks):
            lanes = slice(c * HEAD_DIM, (c + 1) * HEAD_DIM)
            win = u_ref[pl.ds(win0, win_rows), lanes]
            shifted = [win] + [pltpu.roll(win, win_rows - r, 0) for r in range(1, 8)]
            acc = jnp.zeros((CONV_ROWS, HEAD_DIM), F32)
            for k in range(CONV_WIDTH):
                off = k + CONV_GAP - CONV_PAD
                q, r = divmod(off, 8)
                acc = acc + w_ref[k:k + 1, lanes] * shifted[r][8 * q:8 * q + CONV_ROWS, :]
            accs.append(acc + b_ref[:, lanes])
        total = accs[0]
        for acc in accs[1:]:
            total = total + acc
        mu = jnp.sum(total, axis=-1, keepdims=True) * (1.0 / ch)
        cen = [acc - mu for acc in accs]
        sq = cen[0] * cen[0]
        for cc in cen[1:]:
            sq = sq + cc * cc
        rstd = lax.rsqrt(jnp.sum(sq, axis=-1, keepdims=True) * (1.0 / ch) + LN_EPS)
        for c in range(n_lane_blocks):
            lanes = slice(c * HEAD_DIM, (c + 1) * HEAD_DIM)
            y = cen[c] * rstd * lng_ref[:, lanes] + lnb_ref[:, lanes]
            o_ref[0, pl.ds(out_row, CONV_ROWS), lanes] = _silu(y).astype(o_ref.dtype)
        return carry

    lax.fori_loop(0, (n_lat + n_ctx) // CONV_ROWS, conv_rows, 0)


def _conformer(z, dw_w, dw_b, ln_g, ln_b, n_lat):
    n_batch, t, _ = z.shape
    ch = MIX_HALF
    n_ctx = t - n_lat
    pad_rows = t + 3 * CONV_GAP
    row = lambda v: v.reshape(1, ch)
    vec_spec = pl.BlockSpec((1, ch), lambda b: (0, 0))
    return pl.pallas_call(
        functools.partial(_conformer_kernel, n_lat=n_lat, n_ctx=n_ctx),
        grid=(n_batch,),
        in_specs=[
            pl.BlockSpec((1, t, ch), lambda b: (b, 0, 0)),
            pl.BlockSpec((1, t, ch), lambda b: (b, 0, 1)),
            pl.BlockSpec((CONV_WIDTH, ch), lambda b: (0, 0)),
            vec_spec, vec_spec, vec_spec,
        ],
        out_specs=pl.BlockSpec((1, t, ch), lambda b: (b, 0, 0)),
        out_shape=jax.ShapeDtypeStruct((n_batch, t, ch), BF16),
        scratch_shapes=[pltpu.VMEM((pad_rows, ch), F32)],
        compiler_params=_params(1),
        name="conformer_conv",
    )(z, z, dw_w, row(dw_b), row(ln_g), row(ln_b))


def _dft_matrix(n_lat):
    w = jnp.asarray(_dft_matrix_host(n_lat))
    return w.astype(BF16), w.T.astype(BF16)


@functools.lru_cache(maxsize=2)
def _dft_matrix_host(n_lat):
    m = 2 * n_lat
    r = np.arange(m)[:, None]
    t = np.arange(n_lat)[None, :]
    within = r % (2 * FREQ_TILE)
    is_sin = within >= FREQ_TILE
    f = (r // (2 * FREQ_TILE)) * FREQ_TILE + within % FREQ_TILE
    ang = ((f * t) % m) * (2.0 * math.pi / m)
    nyq = np.where(t % 2 == 0, 1.0, -1.0)
    return np.where(is_sin, np.where(f == 0, nyq, np.sin(ang)), np.cos(ang)).astype(np.float32)


def _filter_kernel(feat_ref, w1_ref, b1_ref, w2_ref, b2_ref, w3_ref, b3_ref, fr_ref, w4_ref,
                   t_ref, dl_ref, o_ref, *, n_tiles):
    fr = fr_ref[...]
    h = jnp.sin(fr[0:1, :] * (_dot_hi(feat_ref[...], w1_ref[...]) + b1_ref[...]))
    h = jnp.sin(fr[1:2, :] * (_dot_hi(h, w2_ref[...]) + b2_ref[...]))
    h = jnp.sin(fr[2:3, :] * (_dot_hi(h, w3_ref[...]) + b3_ref[...]))
    hh = _dot_hi(h, w4_ref[...]) * jnp.exp(-t_ref[...] * dl_ref[...])
    hh = hh / (jnp.sum(jnp.abs(hh), axis=0, keepdims=True) + 1e-6)
    is_bwd = pl.program_id(0) >= n_tiles // 2
    row = lax.broadcasted_iota(jnp.int32, hh.shape, 0)
    o_ref[...] = jnp.where(jnp.logical_and(is_bwd, row == 0), 0.0, hh).astype(o_ref.dtype)


def _hyena_filters(n_lat, w1, b1, w2, b2, w3, b3, w4, freq):
    fw = w1.shape[1]
    n = w4.shape[1]
    tn = 512
    bands = (HYENA_POS_DIM - 1) // 2
    t = np.linspace(0.0, 1.0, n_lat)[:, None]
    w = (2.0 * math.pi / n_lat) * np.arange(n_lat)[:, None]
    f = np.linspace(1e-4, bands - 1, bands)[None, :]
    feat = np.concatenate([t, np.cos(f * w), -np.sin(f * w),
                           np.zeros((n_lat, HYENA_POS_PAD - HYENA_POS_DIM))], axis=-1)
    deltas = np.abs(np.linspace(HYENA_MIN_DECAY, HYENA_MAX_DECAY, HYENA_CH))
    deltas = np.tile(deltas, n // HYENA_CH)[None, :]
    w1p = jnp.concatenate([w1, jnp.zeros((HYENA_POS_PAD - HYENA_POS_DIM, fw), F32)], axis=0)
    full = lambda shape: pl.BlockSpec(shape, lambda j: (0, 0))
    return pl.pallas_call(
        functools.partial(_filter_kernel, n_tiles=n // tn),
        grid=(n // tn,),
        in_specs=[
            full((n_lat, HYENA_POS_PAD)), full((HYENA_POS_PAD, fw)), full((1, fw)),
            full((fw, fw)), full((1, fw)), full((fw, fw)), full((1, fw)), full((3, fw)),
            pl.BlockSpec((fw, tn), lambda j: (0, j)),
            full((n_lat, 1)),
            pl.BlockSpec((1, tn), lambda j: (0, j)),
        ],
        out_specs=pl.BlockSpec((n_lat, tn), lambda j: (0, j)),
        out_shape=jax.ShapeDtypeStruct((n_lat, n), BF16),
        compiler_params=_params(1),
        name="hyena_filters",
    )(jnp.asarray(feat, F32), w1p, b1.reshape(1, fw), w2, b2.reshape(1, fw), w3, b3.reshape(1, fw),
      freq, w4, jnp.asarray(t, F32), jnp.asarray(deltas, F32))


def _filter_dft_kernel(wf_ref, hf_ref, hb_ref, ka_ref, kb_ref, k2_ref, *, scale):
    gf = _dot(wf_ref[...], hf_ref[...])
    gb = _dot(wf_ref[...], hb_ref[...])
    kr = (gf[:FREQ_TILE] + gb[:FREQ_TILE]) * scale
    ki = (gb[FREQ_TILE:] - gf[FREQ_TILE:]) * scale
    nyq = (gf[FREQ_TILE:FREQ_TILE + 1] + gb[FREQ_TILE:FREQ_TILE + 1]) * scale
    row = lax.broadcasted_iota(jnp.int32, kr.shape, 0)
    first = jnp.logical_and(row == 0, pl.program_id(0) == 0)
    ka_ref[0] = jnp.where(first, 0.5 * kr, kr)
    kb_ref[0] = jnp.where(first, 0.5 * nyq, kr)
    k2_ref[0] = jnp.where(first, 0.0, ki)


def _filter_dft(wf, hmat, n_lat):
    tn = 512
    n_ct = HYENA_CH // tn
    out = jax.ShapeDtypeStruct((HYENA_ORDER, n_lat, HYENA_CH), F32)
    o_spec = pl.BlockSpec((1, FREQ_TILE, tn), lambda i, n, c: (n, i, c))
    return pl.pallas_call(
        functools.partial(_filter_dft_kernel, scale=1.0 / n_lat),
        grid=(n_lat // FREQ_TILE, HYENA_ORDER, n_ct),
        in_specs=[
            pl.BlockSpec((2 * FREQ_TILE, n_lat), lambda i, n, c: (i, 0)),
            pl.BlockSpec((n_lat, tn), lambda i, n, c: (0, n * n_ct + c)),
            pl.BlockSpec((n_lat, tn), lambda i, n, c: (0, (HYENA_ORDER + n) * n_ct + c)),
        ],
        out_specs=[o_spec, o_spec, o_spec],
        out_shape=[out, out, out],
        compiler_params=_params(3),
        name="hyena_filter_dft",
    )(wf, hmat, hmat)


def _short_conv_kernel(z_ref, w_ref, b_ref, o_ref):
    z = z_ref[0].astype(F32)
    n = z.shape[0]
    row = lax.broadcasted_iota(jnp.int32, z.shape, 0)
    prev = jnp.where(row == 0, 0.0, pltpu.roll(z, 1, 0))
    nxt = jnp.where(row == n - 1, 0.0, pltpu.roll(z, n - 1, 0))
    o = w_ref[0:1, :] * prev + w_ref[1:2, :] * z + w_ref[2:3, :] * nxt + b_ref[...]
    o_ref[0] = o.astype(o_ref.dtype)


def _short_conv(z, w, b, n_lat):
    n_batch = z.shape[0]
    n = w.shape[1]
    tc = 512
    return pl.pallas_call(
        _short_conv_kernel,
        grid=(n_batch, n // tc),
        in_specs=[pl.BlockSpec((1, n_lat, tc), lambda bb, c: (bb, 0, c)),
                  pl.BlockSpec((3, tc), lambda bb, c: (0, c)),
                  pl.BlockSpec((1, tc), lambda bb, c: (0, c))],
        out_specs=pl.BlockSpec((1, n_lat, tc), lambda bb, c: (bb, 0, c)),
        out_shape=jax.ShapeDtypeStruct((n_batch, n_lat, n), BF16),
        compiler_params=_params(2),
        name="hyena_short_conv",
    )(z, w, b.reshape(1, n))


def _spectrum_kernel(wf_ref, u_ref, ka_ref, kb_ref, k2_ref, o_ref):
    g = _dot(wf_ref[...], u_ref[0])
    p, q = g[:FREQ_TILE], g[FREQ_TILE:]
    o_ref[0, :FREQ_TILE, :] = (p * ka_ref[0] + q * k2_ref[0]).astype(o_ref.dtype)
    o_ref[0, FREQ_TILE:, :] = (q * kb_ref[0] - p * k2_ref[0]).astype(o_ref.dtype)


def _spectrum(wf, u, u_col0, spectra, order, n_lat):
    n_batch = u.shape[0]
    tn = 512
    n_ct = HYENA_CH // tn
    c0 = u_col0 // tn
    k_spec = pl.BlockSpec((1, FREQ_TILE, tn), lambda i, c, b: (order, i, c))
    return pl.pallas_call(
        _spectrum_kernel,
        grid=(n_lat // FREQ_TILE, n_ct, n_batch),
        in_specs=[
            pl.BlockSpec((2 * FREQ_TILE, n_lat), lambda i, c, b: (i, 0)),
            pl.BlockSpec((1, n_lat, tn), lambda i, c, b: (b, 0, c0 + c)),
            k_spec, k_spec, k_spec,
        ],
        out_specs=pl.BlockSpec((1, 2 * FREQ_TILE, tn), lambda i, c, b: (b, i, c)),
        out_shape=jax.ShapeDtypeStruct((n_batch, 2 * n_lat, HYENA_CH), BF16),
        compiler_params=_params(3),
        name="hyena_spectrum",
    )(wf, u, *spectra)


def _inverse_gate_kernel(wt_ref, h_ref, u_ref, x_ref, skip_ref, o_ref):
    y = _dot(wt_ref[...], h_ref[0])
    u = u_ref[0].astype(F32)
    o_ref[0] = (x_ref[0].astype(F32) * (y + skip_ref[0] * u)).astype(o_ref.dtype)


def _inverse_gate(wft, h, u, u_col0, x, x_col0, skip, order, n_lat):
    n_batch = h.shape[0]
    tn = 512
    tt = _tile(n_lat, 512)
    n_ct = HYENA_CH // tn
    uc0 = u_col0 // tn
    xc0 = x_col0 // tn
    return pl.pallas_call(
        _inverse_gate_kernel,
        grid=(n_lat // tt, n_ct, n_batch),
        in_specs=[
            pl.BlockSpec((tt, 2 * n_lat), lambda i, c, b: (i, 0)),
            pl.BlockSpec((1, 2 * n_lat, tn), lambda i, c, b: (b, 0, c)),
            pl.BlockSpec((1, tt, tn), lambda i, c, b: (b, i, uc0 + c)),
            pl.BlockSpec((1, tt, tn), lambda i, c, b: (b, i, xc0 + c)),
            pl.BlockSpec((1, 1, tn), lambda i, c, b: (order, 0, c)),
        ],
        out_specs=pl.BlockSpec((1, tt, tn), lambda i, c, b: (b, i, c)),
        out_shape=jax.ShapeDtypeStruct((n_batch, n_lat, HYENA_CH), BF16),
        compiler_params=_params(3),
        name="hyena_inverse_gate",
    )(wft, h, u, x, skip.reshape(HYENA_ORDER, 1, HYENA_CH))


def _hyena(z, short_w, short_b, filt, skip, n_lat):
    wf, wft = _dft_matrix(n_lat)
    hmat = _hyena_filters(n_lat, *filt)
    spectra = _filter_dft(wf, hmat, n_lat)
    zc = _short_conv(z, short_w, short_b, n_lat)
    h0 = _spectrum(wf, zc, 2 * HYENA_CH, spectra, 0, n_lat)
    y1 = _inverse_gate(wft, h0, zc, 2 * HYENA_CH, zc, 0, skip, 0, n_lat)
    h1 = _spectrum(wf, y1, 0, spectra, 1, n_lat)
    return _inverse_gate(wft, h1, y1, 0, zc, HYENA_CH, skip, 1, n_lat)


def _norm_mm_kernel(a_ref, g_ref, w_ref, o_ref):
    a = a_ref[0].astype(F32)
    a = a * lax.rsqrt(jnp.mean(a * a, axis=-1, keepdims=True) + NORM_EPS) * g_ref[...]
    o_ref[0] = _dot(a.astype(BF16), w_ref[...]).astype(o_ref.dtype)


def _norm_mm(z, col0, g, w, rows, tm, name):
    n_batch = z.shape[0]
    k, n = w.shape
    c0 = col0 // k
    tm = _tile(rows, tm)
    return pl.pallas_call(
        _norm_mm_kernel,
        grid=(n_batch, rows // tm),
        in_specs=[pl.BlockSpec((1, tm, k), lambda b, i: (b, i, c0)),
                  pl.BlockSpec((1, k), lambda b, i: (0, 0)),
                  pl.BlockSpec((k, n), lambda b, i: (0, 0))],
        out_specs=pl.BlockSpec((1, tm, n), lambda b, i: (b, i, 0)),
        out_shape=jax.ShapeDtypeStruct((n_batch, rows, n), BF16),
        compiler_params=_params(2),
        name=name,
    )(z, g.reshape(1, k), w)


def _mla_kernel(q_ref, kn_ref, v_ref, kr_ref, cq_ref, uq_ref, dq_ref, ck_ref, uk_ref, dk_ref,
                o_ref, kb_ref, vb_ref):
    j = pl.program_id(2)
    half = MLA_ROPE // 4

    @pl.when(j == 0)
    def _():
        kb_ref[:, :HEAD_DIM] = kn_ref[0]
        kr = kr_ref[0].astype(F32)
        kb_ref[:, HEAD_DIM:] = _rope128(kr, ck_ref[...], uk_ref[...], dk_ref[...], half).astype(BF16)
        _store_v_ones(vb_ref, v_ref[0])

    scale = MLA_QK ** -0.5 * LOG2E
    for r in range(q_ref.shape[1] // ROW_ALIGN):
        rows = slice(r * ROW_ALIGN, (r + 1) * ROW_ALIGN)
        qn = (q_ref[0, rows, :HEAD_DIM].astype(F32) * scale).astype(BF16)
        qr = q_ref[0, rows, HEAD_DIM:].astype(F32)
        qr = (_rope128(qr, cq_ref[rows, :], uq_ref[rows, :], dq_ref[rows, :], half) * scale).astype(BF16)
        s = _nt_dot(jnp.concatenate([qn, qr], axis=1), kb_ref[...])
        o_ref[0, rows, :] = _softmax_pv(s, vb_ref[...]).astype(o_ref.dtype)


def _mla(q, kv, z, tables, n_lat):
    n_batch, t, _ = kv.shape
    tq = _tile(n_lat, 512)
    kr_block = (IN_ODD - MLA_ROPE) // HEAD_DIM
    cos, sin_up, sin_dn = tables
    tq_spec = pl.BlockSpec((tq, HEAD_DIM), lambda b, h, j: (j, 0))
    tk_spec = pl.BlockSpec((t, HEAD_DIM), lambda b, h, j: (0, 0))
    return pl.pallas_call(
        _mla_kernel,
        grid=(n_batch, MLA_HEADS, n_lat // tq),
        in_specs=[
            pl.BlockSpec((1, tq, MLA_QK_PAD), lambda b, h, j: (b, j, h)),
            pl.BlockSpec((1, t, HEAD_DIM), lambda b, h, j: (b, 0, 2 * h)),
            pl.BlockSpec((1, t, HEAD_DIM), lambda b, h, j: (b, 0, 2 * h + 1)),
            pl.BlockSpec((1, t, HEAD_DIM), lambda b, h, j: (b, 0, kr_block)),
            tq_spec, tq_spec, tq_spec, tk_spec, tk_spec, tk_spec,
        ],
        out_specs=pl.BlockSpec((1, tq, HEAD_DIM), lambda b, h, j: (b, j, h)),
        out_shape=jax.ShapeDtypeStruct((n_batch, n_lat, MLA_HEADS * HEAD_DIM), BF16),
        scratch_shapes=[pltpu.VMEM((t, MLA_QK_PAD), BF16), pltpu.VMEM((t, 2 * HEAD_DIM), BF16)],
        compiler_params=_params(3),
        name="mla_attention",
    )(q, kv, kv, z, cos, sin_up, sin_dn, cos, sin_up, sin_dn)


def _pad_uq(w_uq):
    rank = w_uq.shape[0]
    w = w_uq.reshape(rank, MLA_HEADS, MLA_QK)
    w = jnp.concatenate([w, jnp.zeros((rank, MLA_HEADS, MLA_QK_PAD - MLA_QK), w.dtype)], axis=-1)
    return w.reshape(rank, MLA_HEADS * MLA_QK_PAD)


def _ffn(s, rows, norm_g, mod, wg, wu, wd, tm_resid, ctx_rows_from):
    act = _norm_matmul(s, rows, norm_g, mod, 3, 4, [wg, wu], 1024, 512, ctx_rows_from, "ffn_gate_up")
    return _mm_resid([act], [wd], s, mod, 5, rows, tm_resid, 512, ctx_rows_from, "ffn_down")


def kernel(x, c, ctx, c_ctx, ada_w, ada_b, norm_mix_g, norm_ffn_g, e_w_in, e_w_out, e_dw_w, e_dw_b, e_ln_g, e_ln_b, e_qn_g, e_kn_g, o_w_in, o_w_out, o_short_w, o_short_b, o_f_w1, o_f_b1, o_f_w2, o_f_b2, o_f_w3, o_f_b3, o_f_w4, o_f_freq, o_skip, o_q_norm_g, o_kv_norm_g, o_w_uq, o_w_ukv, ffn_w_gate, ffn_w_up, ffn_w_down, final_norm_g):
    n_batch, n_lat, d = x.shape
    n_ctx = ctx.shape[1]
    t = n_lat + n_ctx
    bf = lambda w: w.astype(BF16)

    n_rows = -(-(n_batch + 1) // 8) * 8
    c_rows = jnp.concatenate(
        [c, c_ctx[None, :], jnp.zeros((n_rows - n_batch - 1, d), F32)], axis=0)
    mod = _ada(c_rows, ada_w, ada_b).reshape(ada_w.shape[0], n_rows * 6, 1, d)

    s = jnp.concatenate([x, ctx], axis=1)

    z = _norm_matmul(s, t, norm_mix_g[0], mod[0], 0, 1, [bf(e_w_in[0])], 1024, e_w_in.shape[2] // 2, n_lat,
                     "even_w_in")
    conv = _conformer(z, e_dw_w[0], e_dw_b[0], e_ln_g[0], e_ln_b[0], n_lat)
    att = _gqa(z, _axial_tables(n_lat, n_ctx, HEAD_DIM // 2), e_qn_g[0], e_kn_g[0], n_lat)
    w_out = bf(e_w_out[0])
    s = _mm_resid([conv, att], [w_out[:MIX_HALF], w_out[MIX_HALF:]], s, mod[0], 2, t, 768, 1024,
                  n_lat, "even_w_out")
    s = _ffn(s, t, norm_ffn_g[0], mod[0], bf(ffn_w_gate[0]), bf(ffn_w_up[0]), bf(ffn_w_down[0]),
             768, n_lat)

    w_in = jnp.concatenate([o_w_in[0], jnp.zeros((d, IN_ODD_PAD - IN_ODD), F32)], axis=1)
    z = _norm_matmul(s, t, norm_mix_g[1], mod[1], 0, 1, [bf(w_in)], 1024, IN_ODD_PAD // 4, n_lat, "odd_w_in")
    filt = (o_f_w1[0], o_f_b1[0], o_f_w2[0], o_f_b2[0], o_f_w3[0], o_f_b3[0], o_f_w4[0], o_f_freq[0])
    y_h = _hyena(z, o_short_w[0], o_short_b[0], filt, o_skip[0], n_lat)
    q = _norm_mm(z, HYENA_IN, o_q_norm_g[0], bf(_pad_uq(o_w_uq[0])), n_lat, 1024, "mla_q_up")
    kv = _norm_mm(z, HYENA_IN + MLA_Q_RANK, o_kv_norm_g[0], bf(o_w_ukv[0]), t, 768, "mla_kv_up")
    att = _mla(q, kv, z, _axial_tables(n_lat, n_ctx, MLA_ROPE // 2), n_lat)
    w_out = bf(o_w_out[0])
    xs = _mm_resid([y_h, att], [w_out[:MIX_HALF], w_out[MIX_HALF:]], s, mod[1], 2, n_lat, 1024, 1024,
                   None, "odd_w_out")
    xs = _ffn(xs, n_lat, norm_ffn_g[1], mod[1], bf(ffn_w_gate[1]), bf(ffn_w_up[1]), bf(ffn_w_down[1]),
              512, None)
    return _final_norm(xs, final_norm_g)
```

```python
import functools
import math

import numpy as np
import jax
import jax.numpy as jnp
from jax import lax
from jax.experimental import pallas as pl
from jax.experimental.pallas import tpu as pltpu

F32 = jnp.float32
BF16 = jnp.bfloat16

D_MODEL = 2048
GRID_W = 64
HEAD_DIM = 128
ROPE_THETA = 10000.0
NORM_EPS = 1e-6
LN_EPS = 1e-5
MIX_HALF = D_MODEL // 2
CONV_WIDTH = 31
CONV_PAD = (CONV_WIDTH - 1) // 2
GQA_Q_HEADS = MIX_HALF // HEAD_DIM
GQA_KV_HEADS = GQA_Q_HEADS // 4
GQA_GROUP = GQA_Q_HEADS // GQA_KV_HEADS
HYENA_CH = MIX_HALF
HYENA_ORDER = 2
HYENA_POS_DIM = 33
HYENA_POS_PAD = 40
HYENA_FAST_DECAY = 0.3
HYENA_SLOW_DECAY = 1.5
HYENA_TARGET = 1e-2
HYENA_MAX_DECAY = math.log(HYENA_TARGET) / HYENA_FAST_DECAY
HYENA_MIN_DECAY = math.log(HYENA_TARGET) / HYENA_SLOW_DECAY
HYENA_IN = 3 * HYENA_CH
MLA_HEADS = MIX_HALF // HEAD_DIM
MLA_Q_RANK = D_MODEL // 4
MLA_KV_RANK = D_MODEL // 8
MLA_NOPE = HEAD_DIM
MLA_ROPE = HEAD_DIM // 2
MLA_QK = MLA_NOPE + MLA_ROPE
MLA_QK_PAD = 2 * HEAD_DIM
IN_ODD = HYENA_IN + MLA_Q_RANK + MLA_KV_RANK + MLA_ROPE
IN_ODD_PAD = 4096

V7X_VMEM_LIMIT_BYTES = 56 * 1024 * 1024
FREQ_TILE = 512


def _params(n_axes):
    return pltpu.CompilerParams(
        dimension_semantics=("arbitrary",) * n_axes,
        vmem_limit_bytes=V7X_VMEM_LIMIT_BYTES,
    )


ROW_ALIGN = 256


def _tile(n, target, align=ROW_ALIGN):
    best = None
    for cand in range(align, min(n, target) + 1, align):
        if n % cand == 0:
            best = cand
    assert best is not None, (n, target, align)
    return best


def _dot(a, b):
    return jnp.dot(a, b, preferred_element_type=F32)


def _dot_hi(a, b):
    return jnp.dot(a, b, preferred_element_type=F32, precision=lax.Precision.HIGHEST)


def _silu(v):
    return v * jax.nn.sigmoid(v)


def _ada_kernel(c_ref, w_ref, b_ref, o_ref):
    o_ref[0] = _dot_hi(_silu(c_ref[...]), w_ref[0]) + b_ref[0]


def _ada(c_rows, ada_w, ada_b):
    depth, d, n = ada_w.shape
    r = c_rows.shape[0]
    tn = 512
    return pl.pallas_call(
        _ada_kernel,
        grid=(depth, n // tn),
        in_specs=[
            pl.BlockSpec((r, d), lambda l, j: (0, 0)),
            pl.BlockSpec((1, d, tn), lambda l, j: (l, 0, j)),
            pl.BlockSpec((1, 1, tn), lambda l, j: (l, 0, j)),
        ],
        out_specs=pl.BlockSpec((1, r, tn), lambda l, j: (l, 0, j)),
        out_shape=jax.ShapeDtypeStruct((depth, r, n), F32),
        compiler_params=_params(2),
        name="ada_mod",
    )(c_rows, ada_w, ada_b.reshape(depth, 1, n))


def _norm_matmul_kernel(x_ref, g_ref, shb_ref, scb_ref, shc_ref, scc_ref, *rest, ctx_sub, glu):
    if glu:
        wg_ref, wu_ref, o_ref, h_ref = rest
    else:
        w_ref, o_ref, h_ref = rest

    def project(h):
        if glu:
            return (_silu(_dot(h, wg_ref[...])) * _dot(h, wu_ref[...])).astype(o_ref.dtype)
        return _dot(h, w_ref[...]).astype(o_ref.dtype)

    @pl.when(pl.program_id(2) == 0)
    def _():
        i = pl.program_id(1)
        n_sub = h_ref.shape[0] // ROW_ALIGN
        for s in range(n_sub):
            rows = slice(s * ROW_ALIGN, (s + 1) * ROW_ALIGN)
            x = x_ref[0, rows, :]
            y = x * lax.rsqrt(jnp.mean(x * x, axis=-1, keepdims=True) + NORM_EPS) * g_ref[...]
            shift, scale = shb_ref[0], scb_ref[0]
            if ctx_sub is not None:
                is_ctx = i * n_sub + s >= ctx_sub
                shift = jnp.where(is_ctx, shc_ref[0], shift)
                scale = jnp.where(is_ctx, scc_ref[0], scale)
            h = (y * (1.0 + scale) + shift).astype(h_ref.dtype)
            h_ref[rows, :] = h
            o_ref[0, rows, :] = project(h)

    @pl.when(pl.program_id(2) > 0)
    def _():
        o_ref[0] = project(h_ref[...])


def _norm_matmul(s, rows, g, mod, k_shift, k_scale, weights, tm, tn, ctx_rows_from, name):
    n_batch, _, d = s.shape
    n = weights[0].shape[1]
    tm = _tile(rows, tm)
    ctx_sub = None if ctx_rows_from is None else ctx_rows_from // ROW_ALIGN
    x_spec = pl.BlockSpec((1, tm, d), lambda b, i, j: (b, i, 0))
    batch_mod = lambda k: pl.BlockSpec((1, 1, d), lambda b, i, j: (b * 6 + k, 0, 0))
    ctx_mod = lambda k: pl.BlockSpec((1, 1, d), lambda b, i, j: (n_batch * 6 + k, 0, 0))
    w_spec = pl.BlockSpec((d, tn), lambda b, i, j: (0, j))
    return pl.pallas_call(
        functools.partial(_norm_matmul_kernel, ctx_sub=ctx_sub, glu=len(weights) == 2),
        grid=(n_batch, rows // tm, n // tn),
        in_specs=[x_spec, pl.BlockSpec((1, d), lambda b, i, j: (0, 0)),
                  batch_mod(k_shift), batch_mod(k_scale), ctx_mod(k_shift), ctx_mod(k_scale)]
                 + [w_spec] * len(weights),
        out_specs=pl.BlockSpec((1, tm, tn), lambda b, i, j: (b, i, j)),
        out_shape=jax.ShapeDtypeStruct((n_batch, rows, n), BF16),
        scratch_shapes=[pltpu.VMEM((tm, d), BF16)],
        compiler_params=_params(3),
        name=name,
    )(s, g.reshape(1, d), mod, mod, mod, mod, *weights)


def _final_norm_kernel(x_ref, g_ref, o_ref):
    x = x_ref[0]
    o_ref[0] = x * lax.rsqrt(jnp.mean(x * x, axis=-1, keepdims=True) + NORM_EPS) * g_ref[...]


def _final_norm(s, g):
    n_batch, rows, d = s.shape
    tr = 256
    return pl.pallas_call(
        _final_norm_kernel,
        grid=(n_batch, rows // tr),
        in_specs=[pl.BlockSpec((1, tr, d), lambda b, j: (b, j, 0)),
                  pl.BlockSpec((1, d), lambda b, j: (0, 0))],
        out_specs=pl.BlockSpec((1, tr, d), lambda b, j: (b, j, 0)),
        out_shape=jax.ShapeDtypeStruct((n_batch, rows, d), F32),
        compiler_params=_params(2),
        name="final_norm",
    )(s, g.reshape(1, d))


def _mm_resid_kernel(*refs, n_a, ctx_sub):
    a_refs = refs[:n_a]
    w_refs = refs[n_a:2 * n_a]
    r_ref, gb_ref, gc_ref, o_ref = refs[2 * n_a:]
    acc = _dot(a_refs[0][0], w_refs[0][...])
    for a_ref, w_ref in zip(a_refs[1:], w_refs[1:]):
        acc = acc + _dot(a_ref[0], w_ref[...])
    i = pl.program_id(2)
    n_sub = 1 if ctx_sub is None else acc.shape[0] // ROW_ALIGN
    sub = acc.shape[0] // n_sub
    for s in range(n_sub):
        gate = gb_ref[0]
        if ctx_sub is not None:
            gate = jnp.where(i * n_sub + s >= ctx_sub, gc_ref[0], gate)
        rows = slice(s * sub, (s + 1) * sub)
        o_ref[0, rows, :] = r_ref[0, rows, :] + gate * acc[rows, :]


def _mm_resid(a_list, w, resid, mod, k_gate, rows, tm, tn, ctx_rows_from, name):
    n_batch = resid.shape[0]
    n = w.shape[1]
    k = a_list[0].shape[2]
    assert all(a.shape[2] == k for a in a_list) and w.shape[0] == k * len(a_list)
    tm = _tile(rows, tm)
    ctx_sub = None if ctx_rows_from is None else ctx_rows_from // ROW_ALIGN
    ctx_row = n_batch
    w_spec = lambda part: pl.BlockSpec((k, tn), lambda j, b, i: (part, j))
    in_specs = [pl.BlockSpec((1, tm, k), lambda j, b, i: (b, i, 0)) for _ in a_list]
    in_specs += [w_spec(part) for part in range(len(a_list))]
    in_specs += [
        pl.BlockSpec((1, tm, tn), lambda j, b, i: (b, i, j)),
        pl.BlockSpec((1, 1, tn), lambda j, b, i: (b * 6 + k_gate, 0, j)),
        pl.BlockSpec((1, 1, tn), lambda j, b, i: (ctx_row * 6 + k_gate, 0, j)),
    ]
    return pl.pallas_call(
        functools.partial(_mm_resid_kernel, n_a=len(a_list), ctx_sub=ctx_sub),
        grid=(n // tn, n_batch, rows // tm),
        in_specs=in_specs,
        out_specs=pl.BlockSpec((1, tm, tn), lambda j, b, i: (b, i, j)),
        out_shape=jax.ShapeDtypeStruct((n_batch, rows, n), F32),
        compiler_params=_params(3),
        name=name,
    )(*a_list, *([w] * len(a_list)), resid, mod, mod)


def _axial_tables(n_lat, n_ctx, axis_dims):
    half = axis_dims // 2
    t = np.arange(n_lat)
    pos = np.stack([t // GRID_W, t % GRID_W], axis=1).astype(np.float64)
    inv = ROPE_THETA ** (-np.arange(half, dtype=np.float64) / half)
    lane = np.arange(HEAD_DIM)
    active = lane < 2 * axis_dims
    axis = np.where(active, lane // axis_dims, 0)
    within = lane % axis_dims
    ang = pos[:, axis] * inv[within % half][None, :]
    cos = np.where(active[None, :], np.cos(ang), 1.0)
    sin = np.where(active[None, :], np.sin(ang), 0.0)
    first = (within < half)[None, :]
    sin_up = np.where(first, -sin, 0.0)
    sin_dn = np.where(first, 0.0, sin)
    ident = np.zeros((n_ctx, HEAD_DIM))
    cos = np.concatenate([cos, ident + 1.0], axis=0)
    sin_up = np.concatenate([sin_up, ident], axis=0)
    sin_dn = np.concatenate([sin_dn, ident], axis=0)
    return tuple(jnp.asarray(v, dtype=F32) for v in (cos, sin_up, sin_dn))


def _rope128(v, cos, sin_up, sin_dn, half):
    return (v * cos + pltpu.roll(v, HEAD_DIM - half, 1) * sin_up
            + pltpu.roll(v, half, 1) * sin_dn)


LOG2E = 1.4426950408889634


def _softmax_pv(s, v_ones):
    p = jnp.exp2(s - jnp.max(s, axis=-1, keepdims=True))
    o = _dot(p.astype(BF16), v_ones)
    return o[:, :HEAD_DIM] / o[:, HEAD_DIM:HEAD_DIM + 1]


def _store_v_ones(vb_ref, v):
    vb_ref[:, :HEAD_DIM] = v
    vb_ref[:, HEAD_DIM:] = jnp.ones(v.shape, v.dtype)


def _gqa_kernel(q_ref, k_ref, v_ref, cq_ref, uq_ref, dq_ref, ck_ref, uk_ref, dk_ref,
                qg_ref, kg_ref, o_ref, kb_ref, vb_ref, *, n_lat, ctx_block):
    j = pl.program_id(2)
    half = HEAD_DIM // 4

    @pl.when(j == 0)
    def _():
        k = k_ref[0].astype(F32)
        k = k * lax.rsqrt(jnp.mean(k * k, axis=-1, keepdims=True) + NORM_EPS) * kg_ref[...]
        kb_ref[...] = _rope128(k, ck_ref[...], uk_ref[...], dk_ref[...], half).astype(BF16)
        _store_v_ones(vb_ref, v_ref[0])

    def attend(k, v):
        for g in range(GQA_GROUP):
            cols = slice(g * HEAD_DIM, (g + 1) * HEAD_DIM)
            q = q_ref[0, :, cols].astype(F32)
            q = q * lax.rsqrt(jnp.mean(q * q, axis=-1, keepdims=True) + NORM_EPS) * qg_ref[...]
            q = _rope128(q, cq_ref[...], uq_ref[...], dq_ref[...], half) * (HEAD_DIM ** -0.5 * LOG2E)
            s = lax.dot_general(q.astype(BF16), k, (((1,), (1,)), ((), ())), preferred_element_type=F32)
            o_ref[0, :, cols] = _softmax_pv(s, v).astype(o_ref.dtype)

    @pl.when(j < ctx_block)
    def _():
        attend(kb_ref[...], vb_ref[...])

    @pl.when(j >= ctx_block)
    def _():
        attend(kb_ref[n_lat:, :], vb_ref[n_lat:, :])


def _gqa(z, tables, qn_g, kn_g, n_lat):
    n_batch, t, _ = z.shape
    tq = 256
    gw = GQA_GROUP * HEAD_DIM
    q0 = 2 * MIX_HALF // gw
    k0 = (2 * MIX_HALF + GQA_Q_HEADS * HEAD_DIM) // HEAD_DIM
    v0 = k0 + GQA_KV_HEADS
    cos, sin_up, sin_dn = tables
    tq_spec = pl.BlockSpec((tq, HEAD_DIM), lambda b, h, j: (j, 0))
    tk_spec = pl.BlockSpec((t, HEAD_DIM), lambda b, h, j: (0, 0))
    g_spec = pl.BlockSpec((1, HEAD_DIM), lambda b, h, j: (0, 0))
    return pl.pallas_call(
        functools.partial(_gqa_kernel, n_lat=n_lat, ctx_block=n_lat // tq),
        grid=(n_batch, GQA_KV_HEADS, t // tq),
        in_specs=[
            pl.BlockSpec((1, tq, gw), lambda b, h, j: (b, j, q0 + h)),
            pl.BlockSpec((1, t, HEAD_DIM), lambda b, h, j: (b, 0, k0 + h)),
            pl.BlockSpec((1, t, HEAD_DIM), lambda b, h, j: (b, 0, v0 + h)),
            tq_spec, tq_spec, tq_spec, tk_spec, tk_spec, tk_spec, g_spec, g_spec,
        ],
        out_specs=pl.BlockSpec((1, tq, gw), lambda b, h, j: (b, j, h)),
        out_shape=jax.ShapeDtypeStruct((n_batch, t, GQA_Q_HEADS * HEAD_DIM), BF16),
        scratch_shapes=[pltpu.VMEM((t, HEAD_DIM), BF16), pltpu.VMEM((t, 2 * HEAD_DIM), BF16)],
        compiler_params=_params(3),
        name="gqa_attention",
    )(z, z, z, cos, sin_up, sin_dn, cos, sin_up, sin_dn,
      qn_g.reshape(1, HEAD_DIM), kn_g.reshape(1, HEAD_DIM))


CONV_ROWS = 32
CONV_HALO = 32
CONV_GAP = 16


def _conformer_kernel(a_ref, g_ref, w_ref, b_ref, lng_ref, lnb_ref, o_ref, u_ref, *, n_lat, n_ctx):
    ch = a_ref.shape[2]
    n_lane_blocks = ch // HEAD_DIM
    lat0 = CONV_GAP
    ctx0 = 2 * CONV_GAP + n_lat
    zeros = jnp.zeros((CONV_GAP, ch), F32)
    u_ref[0:CONV_GAP, :] = zeros
    u_ref[lat0 + n_lat:ctx0, :] = zeros
    u_ref[ctx0 + n_ctx:ctx0 + n_ctx + CONV_GAP, :] = zeros

    glu_rows = 256

    def glu(i, carry):
        src = pl.multiple_of(i * glu_rows, glu_rows)
        dst = pl.multiple_of(jnp.where(src >= n_lat, src + 2 * CONV_GAP, src + CONV_GAP), 8)
        a = a_ref[0, pl.ds(src, glu_rows), :].astype(F32)
        g = g_ref[0, pl.ds(src, glu_rows), :].astype(F32)
        u_ref[pl.ds(dst, glu_rows), :] = a * jax.nn.sigmoid(g)
        return carry

    lax.fori_loop(0, (n_lat + n_ctx) // glu_rows, glu, 0)

    win_rows = CONV_ROWS + CONV_HALO

    def conv_rows(i, carry):
        out_row = pl.multiple_of(i * CONV_ROWS, CONV_ROWS)
        win0 = pl.multiple_of(jnp.where(out_row >= n_lat, out_row + CONV_GAP, out_row), 8)
        accs = []
        for c in range(n_lane_blocks):
            lanes = slice(c * HEAD_DIM, (c + 1) * HEAD_DIM)
            win = u_ref[pl.ds(win0, win_rows), lanes]
            shifted = [win] + [pltpu.roll(win, win_rows - r, 0) for r in range(1, 8)]
            acc = jnp.zeros((CONV_ROWS, HEAD_DIM), F32)
            for k in range(CONV_WIDTH):
                off = k + CONV_GAP - CONV_PAD
                q, r = divmod(off, 8)
                acc = acc + w_ref[k:k + 1, lanes] * shifted[r][8 * q:8 * q + CONV_ROWS, :]
            accs.append(acc + b_ref[:, lanes])
        total = accs[0]
        for acc in accs[1:]:
            total = total + acc
        mu = jnp.sum(total, axis=-1, keepdims=True) * (1.0 / ch)
        cen = [acc - mu for acc in accs]
        sq = cen[0] * cen[0]
        for cc in cen[1:]:
            sq = sq + cc * cc
        rstd = lax.rsqrt(jnp.sum(sq, axis=-1, keepdims=True) * (1.0 / ch) + LN_EPS)
        for c in range(n_lane_blocks):
            lanes = slice(c * HEAD_DIM, (c + 1) * HEAD_DIM)
            y = cen[c] * rstd * lng_ref[:, lanes] + lnb_ref[:, lanes]
            o_ref[0, pl.ds(out_row, CONV_ROWS), lanes] = _silu(y).astype(o_ref.dtype)
        return carry

    lax.fori_loop(0, (n_lat + n_ctx) // CONV_ROWS, conv_rows, 0)


def _conformer(z, dw_w, dw_b, ln_g, ln_b, n_lat):
    n_batch, t, _ = z.shape
    ch = MIX_HALF
    n_ctx = t - n_lat
    pad_rows = t + 3 * CONV_GAP
    row = lambda v: v.reshape(1, ch)
    vec_spec = pl.BlockSpec((1, ch), lambda b: (0, 0))
    return pl.pallas_call(
        functools.partial(_conformer_kernel, n_lat=n_lat, n_ctx=n_ctx),
        grid=(n_batch,),
        in_specs=[
            pl.BlockSpec((1, t, ch), lambda b: (b, 0, 0)),
            pl.BlockSpec((1, t, ch), lambda b: (b, 0, 1)),
            pl.BlockSpec((CONV_WIDTH, ch), lambda b: (0, 0)),
            vec_spec, vec_spec, vec_spec,
        ],
        out_specs=pl.BlockSpec((1, t, ch), lambda b: (b, 0, 0)),
        out_shape=jax.ShapeDtypeStruct((n_batch, t, ch), BF16),
        scratch_shapes=[pltpu.VMEM((pad_rows, ch), F32)],
        compiler_params=_params(1),
        name="conformer_conv",
    )(z, z, dw_w, row(dw_b), row(ln_g), row(ln_b))


def _dft_matrix(n_lat):
    w = jnp.asarray(_dft_matrix_host(n_lat))
    return w.astype(BF16), w.T.astype(BF16)


@functools.lru_cache(maxsize=2)
def _dft_matrix_host(n_lat):
    m = 2 * n_lat
    r = np.arange(m)[:, None]
    t = np.arange(n_lat)[None, :]
    within = r % (2 * FREQ_TILE)
    is_sin = within >= FREQ_TILE
    f = (r // (2 * FREQ_TILE)) * FREQ_TILE + within % FREQ_TILE
    ang = ((f * t) % m) * (2.0 * math.pi / m)
    nyq = np.where(t % 2 == 0, 1.0, -1.0)
    return np.where(is_sin, np.where(f == 0, nyq, np.sin(ang)), np.cos(ang)).astype(np.float32)


def _filter_kernel(feat_ref, w1_ref, b1_ref, w2_ref, b2_ref, w3_ref, b3_ref, fr_ref, w4_ref,
                   t_ref, dl_ref, o_ref, h_ref, *, n_tiles):
    @pl.when(pl.program_id(0) == 0)
    def _():
        fr = fr_ref[...]
        h = jnp.sin(fr[0:1, :] * (_dot_hi(feat_ref[...], w1_ref[...]) + b1_ref[...]))
        h = jnp.sin(fr[1:2, :] * (_dot_hi(h, w2_ref[...]) + b2_ref[...]))
        h_ref[...] = jnp.sin(fr[2:3, :] * (_dot_hi(h, w3_ref[...]) + b3_ref[...]))

    hh = _dot_hi(h_ref[...], w4_ref[...]) * jnp.exp(-t_ref[...] * dl_ref[...])
    hh = hh / (jnp.sum(jnp.abs(hh), axis=0, keepdims=True) + 1e-6)
    is_bwd = pl.program_id(0) >= n_tiles // 2
    row = lax.broadcasted_iota(jnp.int32, hh.shape, 0)
    o_ref[...] = jnp.where(jnp.logical_and(is_bwd, row == 0), 0.0, hh).astype(o_ref.dtype)


def _hyena_filters(n_lat, w1, b1, w2, b2, w3, b3, w4, freq):
    fw = w1.shape[1]
    n = w4.shape[1]
    tn = 512
    bands = (HYENA_POS_DIM - 1) // 2
    t = np.linspace(0.0, 1.0, n_lat)[:, None]
    w = (2.0 * math.pi / n_lat) * np.arange(n_lat)[:, None]
    f = np.linspace(1e-4, bands - 1, bands)[None, :]
    feat = np.concatenate([t, np.cos(f * w), -np.sin(f * w),
                           np.zeros((n_lat, HYENA_POS_PAD - HYENA_POS_DIM))], axis=-1)
    deltas = np.abs(np.linspace(HYENA_MIN_DECAY, HYENA_MAX_DECAY, HYENA_CH))
    deltas = np.tile(deltas, n // HYENA_CH)[None, :]
    w1p = jnp.concatenate([w1, jnp.zeros((HYENA_POS_PAD - HYENA_POS_DIM, fw), F32)], axis=0)
    full = lambda shape: pl.BlockSpec(shape, lambda j: (0, 0))
    return pl.pallas_call(
        functools.partial(_filter_kernel, n_tiles=n // tn),
        grid=(n // tn,),
        in_specs=[
            full((n_lat, HYENA_POS_PAD)), full((HYENA_POS_PAD, fw)), full((1, fw)),
            full((fw, fw)), full((1, fw)), full((fw, fw)), full((1, fw)), full((3, fw)),
            pl.BlockSpec((fw, tn), lambda j: (0, j)),
            full((n_lat, 1)),
            pl.BlockSpec((1, tn), lambda j: (0, j)),
        ],
        out_specs=pl.BlockSpec((n_lat, tn), lambda j: (0, j)),
        out_shape=jax.ShapeDtypeStruct((n_lat, n), BF16),
        scratch_shapes=[pltpu.VMEM((n_lat, fw), F32)],
        compiler_params=_params(1),
        name="hyena_filters",
    )(jnp.asarray(feat, F32), w1p, b1.reshape(1, fw), w2, b2.reshape(1, fw), w3, b3.reshape(1, fw),
      freq, w4, jnp.asarray(t, F32), jnp.asarray(deltas, F32))


def _filter_dft_kernel(wf_ref, hf_ref, hb_ref, ka_ref, kb_ref, k2_ref, *, scale):
    gf = _dot(wf_ref[...], hf_ref[...])
    gb = _dot(wf_ref[...], hb_ref[...])
    kr = (gf[:FREQ_TILE] + gb[:FREQ_TILE]) * scale
    ki = (gb[FREQ_TILE:] - gf[FREQ_TILE:]) * scale
    nyq = (gf[FREQ_TILE:FREQ_TILE + 1] + gb[FREQ_TILE:FREQ_TILE + 1]) * scale
    row = lax.broadcasted_iota(jnp.int32, kr.shape, 0)
    first = jnp.logical_and(row == 0, pl.program_id(0) == 0)
    ka_ref[0] = jnp.where(first, 0.5 * kr, kr)
    kb_ref[0] = jnp.where(first, 0.5 * nyq, kr)
    k2_ref[0] = jnp.where(first, 0.0, ki)


def _filter_dft(wf, hmat, n_lat):
    tn = 512
    n_ct = HYENA_CH // tn
    out = jax.ShapeDtypeStruct((HYENA_ORDER, n_lat, HYENA_CH), F32)
    o_spec = pl.BlockSpec((1, FREQ_TILE, tn), lambda i, n, c: (n, i, c))
    return pl.pallas_call(
        functools.partial(_filter_dft_kernel, scale=1.0 / n_lat),
        grid=(n_lat // FREQ_TILE, HYENA_ORDER, n_ct),
        in_specs=[
            pl.BlockSpec((2 * FREQ_TILE, n_lat), lambda i, n, c: (i, 0)),
            pl.BlockSpec((n_lat, tn), lambda i, n, c: (0, n * n_ct + c)),
            pl.BlockSpec((n_lat, tn), lambda i, n, c: (0, (HYENA_ORDER + n) * n_ct + c)),
        ],
        out_specs=[o_spec, o_spec, o_spec],
        out_shape=[out, out, out],
        compiler_params=_params(3),
        name="hyena_filter_dft",
    )(wf, hmat, hmat)


def _short_conv_kernel(z_ref, w_ref, b_ref, o_ref):
    z = z_ref[0].astype(F32)
    n = z.shape[0]
    row = lax.broadcasted_iota(jnp.int32, z.shape, 0)
    prev = jnp.where(row == 0, 0.0, pltpu.roll(z, 1, 0))
    nxt = jnp.where(row == n - 1, 0.0, pltpu.roll(z, n - 1, 0))
    o = w_ref[0:1, :] * prev + w_ref[1:2, :] * z + w_ref[2:3, :] * nxt + b_ref[...]
    o_ref[0] = o.astype(o_ref.dtype)


def _short_conv(z, w, b, n_lat):
    n_batch = z.shape[0]
    n = w.shape[1]
    tc = 512
    return pl.pallas_call(
        _short_conv_kernel,
        grid=(n_batch, n // tc),
        in_specs=[pl.BlockSpec((1, n_lat, tc), lambda bb, c: (bb, 0, c)),
                  pl.BlockSpec((3, tc), lambda bb, c: (0, c)),
                  pl.BlockSpec((1, tc), lambda bb, c: (0, c))],
        out_specs=pl.BlockSpec((1, n_lat, tc), lambda bb, c: (bb, 0, c)),
        out_shape=jax.ShapeDtypeStruct((n_batch, n_lat, n), BF16),
        compiler_params=_params(2),
        name="hyena_short_conv",
    )(z, w, b.reshape(1, n))


def _spectrum_kernel(wf_ref, u_ref, ka_ref, kb_ref, k2_ref, o_ref):
    g = _dot(wf_ref[...], u_ref[0])
    p, q = g[:FREQ_TILE], g[FREQ_TILE:]
    o_ref[0, :FREQ_TILE, :] = (p * ka_ref[0] + q * k2_ref[0]).astype(o_ref.dtype)
    o_ref[0, FREQ_TILE:, :] = (q * kb_ref[0] - p * k2_ref[0]).astype(o_ref.dtype)


def _spectrum(wf, u, u_col0, spectra, order, n_lat):
    n_batch = u.shape[0]
    tn = HYENA_CH
    n_ct = HYENA_CH // tn
    c0 = u_col0 // tn
    k_spec = pl.BlockSpec((1, FREQ_TILE, tn), lambda i, c, b: (order, i, c))
    return pl.pallas_call(
        _spectrum_kernel,
        grid=(n_lat // FREQ_TILE, n_ct, n_batch),
        in_specs=[
            pl.BlockSpec((2 * FREQ_TILE, n_lat), lambda i, c, b: (i, 0)),
            pl.BlockSpec((1, n_lat, tn), lambda i, c, b: (b, 0, c0 + c)),
            k_spec, k_spec, k_spec,
        ],
        out_specs=pl.BlockSpec((1, 2 * FREQ_TILE, tn), lambda i, c, b: (b, i, c)),
        out_shape=jax.ShapeDtypeStruct((n_batch, 2 * n_lat, HYENA_CH), BF16),
        compiler_params=_params(3),
        name="hyena_spectrum",
    )(wf, u, *spectra)


def _inverse_gate_kernel(wt_ref, h_ref, u_ref, x_ref, skip_ref, o_ref):
    y = _dot(wt_ref[...], h_ref[0])
    u = u_ref[0].astype(F32)
    o_ref[0] = (x_ref[0].astype(F32) * (y + skip_ref[0] * u)).astype(o_ref.dtype)


def _inverse_gate(wft, h, u, u_col0, x, x_col0, skip, order, n_lat):
    n_batch = h.shape[0]
    tn = HYENA_CH
    tt = _tile(n_lat, 512)
    n_ct = HYENA_CH // tn
    uc0 = u_col0 // tn
    xc0 = x_col0 // tn
    return pl.pallas_call(
        _inverse_gate_kernel,
        grid=(n_lat // tt, n_ct, n_batch),
        in_specs=[
            pl.BlockSpec((tt, 2 * n_lat), lambda i, c, b: (i, 0)),
            pl.BlockSpec((1, 2 * n_lat, tn), lambda i, c, b: (b, 0, c)),
            pl.BlockSpec((1, tt, tn), lambda i, c, b: (b, i, uc0 + c)),
            pl.BlockSpec((1, tt, tn), lambda i, c, b: (b, i, xc0 + c)),
            pl.BlockSpec((1, 1, tn), lambda i, c, b: (order, 0, c)),
        ],
        out_specs=pl.BlockSpec((1, tt, tn), lambda i, c, b: (b, i, c)),
        out_shape=jax.ShapeDtypeStruct((n_batch, n_lat, HYENA_CH), BF16),
        compiler_params=_params(3),
        name="hyena_inverse_gate",
    )(wft, h, u, x, skip.reshape(HYENA_ORDER, 1, HYENA_CH))


def _hyena(z, short_w, short_b, filt, skip, n_lat):
    wf, wft = _dft_matrix(n_lat)
    hmat = _hyena_filters(n_lat, *filt)
    spectra = _filter_dft(wf, hmat, n_lat)
    zc = _short_conv(z, short_w, short_b, n_lat)
    h0 = _spectrum(wf, zc, 2 * HYENA_CH, spectra, 0, n_lat)
    y1 = _inverse_gate(wft, h0, zc, 2 * HYENA_CH, zc, 0, skip, 0, n_lat)
    h1 = _spectrum(wf, y1, 0, spectra, 1, n_lat)
    return _inverse_gate(wft, h1, y1, 0, zc, HYENA_CH, skip, 1, n_lat)


def _norm_mm_kernel(a_ref, g_ref, w_ref, o_ref):
    a = a_ref[0].astype(F32)
    a = a * lax.rsqrt(jnp.mean(a * a, axis=-1, keepdims=True) + NORM_EPS) * g_ref[...]
    o_ref[0] = _dot(a.astype(BF16), w_ref[...]).astype(o_ref.dtype)


def _norm_mm(z, col0, g, w, rows, tm, name):
    n_batch = z.shape[0]
    k, n = w.shape
    c0 = col0 // k
    tm = _tile(rows, tm)
    return pl.pallas_call(
        _norm_mm_kernel,
        grid=(n_batch, rows // tm),
        in_specs=[pl.BlockSpec((1, tm, k), lambda b, i: (b, i, c0)),
                  pl.BlockSpec((1, k), lambda b, i: (0, 0)),
                  pl.BlockSpec((k, n), lambda b, i: (0, 0))],
        out_specs=pl.BlockSpec((1, tm, n), lambda b, i: (b, i, 0)),
        out_shape=jax.ShapeDtypeStruct((n_batch, rows, n), BF16),
        compiler_params=_params(2),
        name=name,
    )(z, g.reshape(1, k), w)


MLA_ROWS = 512


def _mla_kernel(q_ref, kn_ref, v_ref, kr_ref, cq_ref, uq_ref, dq_ref, ck_ref, uk_ref, dk_ref,
                o_ref, kb_ref, vb_ref):
    j = pl.program_id(2)
    half = MLA_ROPE // 4

    @pl.when(j == 0)
    def _():
        kb_ref[:HEAD_DIM, :] = kn_ref[0].astype(F32).T.astype(BF16)
        kr = kr_ref[0].astype(F32)
        kb_ref[HEAD_DIM:, :] = _rope128(kr, ck_ref[...], uk_ref[...], dk_ref[...], half).T.astype(BF16)
        _store_v_ones(vb_ref, v_ref[0])

    scale = MLA_QK ** -0.5 * LOG2E
    for r in range(q_ref.shape[1] // MLA_ROWS):
        rows = slice(r * MLA_ROWS, (r + 1) * MLA_ROWS)
        qn = (q_ref[0, rows, :HEAD_DIM].astype(F32) * scale).astype(BF16)
        qr = q_ref[0, rows, HEAD_DIM:].astype(F32)
        qr = (_rope128(qr, cq_ref[rows, :], uq_ref[rows, :], dq_ref[rows, :], half) * scale).astype(BF16)
        s = _dot(jnp.concatenate([qn, qr], axis=1), kb_ref[...])
        o_ref[0, rows, :] = _softmax_pv(s, vb_ref[...]).astype(o_ref.dtype)


def _mla(q, kv, z, tables, n_lat):
    n_batch, t, _ = kv.shape
    tq = _tile(n_lat, 2 * MLA_ROWS, MLA_ROWS)
    kr_block = (IN_ODD - MLA_ROPE) // HEAD_DIM
    cos, sin_up, sin_dn = tables
    tq_spec = pl.BlockSpec((tq, HEAD_DIM), lambda b, h, j: (j, 0))
    tk_spec = pl.BlockSpec((t, HEAD_DIM), lambda b, h, j: (0, 0))
    return pl.pallas_call(
        _mla_kernel,
        grid=(n_batch, MLA_HEADS, n_lat // tq),
        in_specs=[
            pl.BlockSpec((1, tq, MLA_QK_PAD), lambda b, h, j: (b, j, h)),
            pl.BlockSpec((1, t, HEAD_DIM), lambda b, h, j: (b, 0, 2 * h)),
            pl.BlockSpec((1, t, HEAD_DIM), lambda b, h, j: (b, 0, 2 * h + 1)),
            pl.BlockSpec((1, t, HEAD_DIM), lambda b, h, j: (b, 0, kr_block)),
            tq_spec, tq_spec, tq_spec, tk_spec, tk_spec, tk_spec,
        ],
        out_specs=pl.BlockSpec((1, tq, HEAD_DIM), lambda b, h, j: (b, j, h)),
        out_shape=jax.ShapeDtypeStruct((n_batch, n_lat, MLA_HEADS * HEAD_DIM), BF16),
        scratch_shapes=[pltpu.VMEM((MLA_QK_PAD, t), BF16), pltpu.VMEM((t, 2 * HEAD_DIM), BF16)],
        compiler_params=_params(3),
        name="mla_attention",
    )(q, kv, kv, z, cos, sin_up, sin_dn, cos, sin_up, sin_dn)


def _pad_uq(w_uq):
    rank = w_uq.shape[0]
    w = w_uq.reshape(rank, MLA_HEADS, MLA_QK)
    w = jnp.concatenate([w, jnp.zeros((rank, MLA_HEADS, MLA_QK_PAD - MLA_QK), w.dtype)], axis=-1)
    return w.reshape(rank, MLA_HEADS * MLA_QK_PAD)


def _ffn(s, rows, norm_g, mod, wg, wu, wd, tm_resid, ctx_rows_from):
    act = _norm_matmul(s, rows, norm_g, mod, 3, 4, [wg, wu], 1024, 512, ctx_rows_from, "ffn_gate_up")
    return _mm_resid([act], wd, s, mod, 5, rows, tm_resid, 512, ctx_rows_from, "ffn_down")


def kernel(x, c, ctx, c_ctx, ada_w, ada_b, norm_mix_g, norm_ffn_g, e_w_in, e_w_out, e_dw_w, e_dw_b, e_ln_g, e_ln_b, e_qn_g, e_kn_g, o_w_in, o_w_out, o_short_w, o_short_b, o_f_w1, o_f_b1, o_f_w2, o_f_b2, o_f_w3, o_f_b3, o_f_w4, o_f_freq, o_skip, o_q_norm_g, o_kv_norm_g, o_w_uq, o_w_ukv, ffn_w_gate, ffn_w_up, ffn_w_down, final_norm_g):
    n_batch, n_lat, d = x.shape
    n_ctx = ctx.shape[1]
    t = n_lat + n_ctx
    bf = lambda w: w.astype(BF16)

    n_rows = -(-(n_batch + 1) // 8) * 8
    c_rows = jnp.concatenate(
        [c, c_ctx[None, :], jnp.zeros((n_rows - n_batch - 1, d), F32)], axis=0)
    mod = _ada(c_rows, ada_w, ada_b).reshape(ada_w.shape[0], n_rows * 6, 1, d)

    s = jnp.concatenate([x, ctx], axis=1)

    z = _norm_matmul(s, t, norm_mix_g[0], mod[0], 0, 1, [bf(e_w_in[0])], 1024, e_w_in.shape[2] // 2, n_lat,
                     "even_w_in")
    conv = _conformer(z, e_dw_w[0], e_dw_b[0], e_ln_g[0], e_ln_b[0], n_lat)
    att = _gqa(z, _axial_tables(n_lat, n_ctx, HEAD_DIM // 2), e_qn_g[0], e_kn_g[0], n_lat)
    s = _mm_resid([conv, att], bf(e_w_out[0]), s, mod[0], 2, t, 768, 1024,
                  n_lat, "even_w_out")
    s = _ffn(s, t, norm_ffn_g[0], mod[0], bf(ffn_w_gate[0]), bf(ffn_w_up[0]), bf(ffn_w_down[0]),
             768, n_lat)

    w_in = jnp.concatenate([o_w_in[0], jnp.zeros((d, IN_ODD_PAD - IN_ODD), F32)], axis=1)
    z = _norm_matmul(s, t, norm_mix_g[1], mod[1], 0, 1, [bf(w_in)], 1024, IN_ODD_PAD // 4, n_lat, "odd_w_in")
    filt = (o_f_w1[0], o_f_b1[0], o_f_w2[0], o_f_b2[0], o_f_w3[0], o_f_b3[0], o_f_w4[0], o_f_freq[0])
    y_h = _hyena(z, o_short_w[0], o_short_b[0], filt, o_skip[0], n_lat)
    q = _norm_mm(z, HYENA_IN, o_q_norm_g[0], bf(_pad_uq(o_w_uq[0])), n_lat, 1024, "mla_q_up")
    kv = _norm_mm(z, HYENA_IN + MLA_Q_RANK, o_kv_norm_g[0], bf(o_w_ukv[0]), t, 768, "mla_kv_up")
    att = _mla(q, kv, z, _axial_tables(n_lat, n_ctx, MLA_ROPE // 2), n_lat)
    xs = _mm_resid([y_h, att], bf(o_w_out[0]), s, mod[1], 2, n_lat, 1024, 1024,
                   None, "odd_w_out")
    xs = _ffn(xs, n_lat, norm_ffn_g[1], mod[1], bf(ffn_w_gate[1]), bf(ffn_w_up[1]), bf(ffn_w_down[1]),
              512, None)
    return _final_norm(xs, final_norm_g)
```

```python
import functools
import math

import numpy as np
import jax
import jax.numpy as jnp
from jax import lax
from jax.experimental import pallas as pl
from jax.experimental.pallas import tpu as pltpu

F32 = jnp.float32
BF16 = jnp.bfloat16

D_MODEL = 2048
GRID_W = 64
HEAD_DIM = 128
ROPE_THETA = 10000.0
NORM_EPS = 1e-6
LN_EPS = 1e-5
MIX_HALF = D_MODEL // 2
CONV_WIDTH = 31
CONV_PAD = (CONV_WIDTH - 1) // 2
GQA_Q_HEADS = MIX_HALF // HEAD_DIM
GQA_KV_HEADS = GQA_Q_HEADS // 4
GQA_GROUP = GQA_Q_HEADS // GQA_KV_HEADS
HYENA_CH = MIX_HALF
HYENA_ORDER = 2
HYENA_POS_DIM = 33
HYENA_POS_PAD = 40
HYENA_FAST_DECAY = 0.3
HYENA_SLOW_DECAY = 1.5
HYENA_TARGET = 1e-2
HYENA_MAX_DECAY = math.log(HYENA_TARGET) / HYENA_FAST_DECAY
HYENA_MIN_DECAY = math.log(HYENA_TARGET) / HYENA_SLOW_DECAY
HYENA_IN = 3 * HYENA_CH
MLA_HEADS = MIX_HALF // HEAD_DIM
MLA_Q_RANK = D_MODEL // 4
MLA_KV_RANK = D_MODEL // 8
MLA_NOPE = HEAD_DIM
MLA_ROPE = HEAD_DIM // 2
MLA_QK = MLA_NOPE + MLA_ROPE
MLA_QK_PAD = 2 * HEAD_DIM
IN_ODD = HYENA_IN + MLA_Q_RANK + MLA_KV_RANK + MLA_ROPE
IN_ODD_PAD = 4096

V7X_VMEM_LIMIT_BYTES = 56 * 1024 * 1024
FREQ_TILE = 512


def _params(n_axes):
    return pltpu.CompilerParams(
        dimension_semantics=("arbitrary",) * n_axes,
        vmem_limit_bytes=V7X_VMEM_LIMIT_BYTES,
    )


ROW_ALIGN = 256


def _tile(n, target, align=ROW_ALIGN):
    best = None
    for cand in range(align, min(n, target) + 1, align):
        if n % cand == 0:
            best = cand
    assert best is not None, (n, target, align)
    return best


def _dot(a, b):
    return jnp.dot(a, b, preferred_element_type=F32)


def _dot_hi(a, b):
    return jnp.dot(a, b, preferred_element_type=F32, precision=lax.Precision.HIGHEST)


def _silu(v):
    return v * jax.nn.sigmoid(v)


def _ada_kernel(c_ref, w_ref, b_ref, o_ref):
    o_ref[0] = _dot_hi(_silu(c_ref[...]), w_ref[0]) + b_ref[0]


def _ada(c_rows, ada_w, ada_b):
    depth, d, n = ada_w.shape
    r = c_rows.shape[0]
    tn = 512
    return pl.pallas_call(
        _ada_kernel,
        grid=(depth, n // tn),
        in_specs=[
            pl.BlockSpec((r, d), lambda l, j: (0, 0)),
            pl.BlockSpec((1, d, tn), lambda l, j: (l, 0, j)),
            pl.BlockSpec((1, 1, tn), lambda l, j: (l, 0, j)),
        ],
        out_specs=pl.BlockSpec((1, r, tn), lambda l, j: (l, 0, j)),
        out_shape=jax.ShapeDtypeStruct((depth, r, n), F32),
        compiler_params=_params(2),
        name="ada_mod",
    )(c_rows, ada_w, ada_b.reshape(depth, 1, n))


def _norm_matmul_kernel(x_ref, g_ref, shb_ref, scb_ref, shc_ref, scc_ref, *rest, ctx_sub, glu):
    if glu:
        wg_ref, wu_ref, o_ref, h_ref = rest
    else:
        w_ref, o_ref, h_ref = rest

    def project(h):
        if glu:
            return (_silu(_dot(h, wg_ref[...])) * _dot(h, wu_ref[...])).astype(o_ref.dtype)
        return _dot(h, w_ref[...]).astype(o_ref.dtype)

    @pl.when(pl.program_id(2) == 0)
    def _():
        i = pl.program_id(1)
        n_sub = h_ref.shape[0] // ROW_ALIGN
        for s in range(n_sub):
            rows = slice(s * ROW_ALIGN, (s + 1) * ROW_ALIGN)
            x = x_ref[0, rows, :]
            y = x * lax.rsqrt(jnp.mean(x * x, axis=-1, keepdims=True) + NORM_EPS) * g_ref[...]
            shift, scale = shb_ref[0], scb_ref[0]
            if ctx_sub is not None:
                is_ctx = i * n_sub + s >= ctx_sub
                shift = jnp.where(is_ctx, shc_ref[0], shift)
                scale = jnp.where(is_ctx, scc_ref[0], scale)
            h = (y * (1.0 + scale) + shift).astype(h_ref.dtype)
            h_ref[rows, :] = h
            o_ref[0, rows, :] = project(h)

    @pl.when(pl.program_id(2) > 0)
    def _():
        o_ref[0] = project(h_ref[...])


def _norm_matmul(s, rows, g, mod, k_shift, k_scale, weights, tm, tn, ctx_rows_from, name):
    n_batch, _, d = s.shape
    n = weights[0].shape[1]
    tm = _tile(rows, tm)
    ctx_sub = None if ctx_rows_from is None else ctx_rows_from // ROW_ALIGN
    x_spec = pl.BlockSpec((1, tm, d), lambda b, i, j: (b, i, 0))
    batch_mod = lambda k: pl.BlockSpec((1, 1, d), lambda b, i, j: (b * 6 + k, 0, 0))
    ctx_mod = lambda k: pl.BlockSpec((1, 1, d), lambda b, i, j: (n_batch * 6 + k, 0, 0))
    w_spec = pl.BlockSpec((d, tn), lambda b, i, j: (0, j))
    return pl.pallas_call(
        functools.partial(_norm_matmul_kernel, ctx_sub=ctx_sub, glu=len(weights) == 2),
        grid=(n_batch, rows // tm, n // tn),
        in_specs=[x_spec, pl.BlockSpec((1, d), lambda b, i, j: (0, 0)),
                  batch_mod(k_shift), batch_mod(k_scale), ctx_mod(k_shift), ctx_mod(k_scale)]
                 + [w_spec] * len(weights),
        out_specs=pl.BlockSpec((1, tm, tn), lambda b, i, j: (b, i, j)),
        out_shape=jax.ShapeDtypeStruct((n_batch, rows, n), BF16),
        scratch_shapes=[pltpu.VMEM((tm, d), BF16)],
        compiler_params=_params(3),
        name=name,
    )(s, g.reshape(1, d), mod, mod, mod, mod, *weights)


def _final_norm_kernel(x_ref, g_ref, o_ref):
    x = x_ref[0]
    o_ref[0] = x * lax.rsqrt(jnp.mean(x * x, axis=-1, keepdims=True) + NORM_EPS) * g_ref[...]


def _final_norm(s, g):
    n_batch, rows, d = s.shape
    tr = 256
    return pl.pallas_call(
        _final_norm_kernel,
        grid=(n_batch, rows // tr),
        in_specs=[pl.BlockSpec((1, tr, d), lambda b, j: (b, j, 0)),
                  pl.BlockSpec((1, d), lambda b, j: (0, 0))],
        out_specs=pl.BlockSpec((1, tr, d), lambda b, j: (b, j, 0)),
        out_shape=jax.ShapeDtypeStruct((n_batch, rows, d), F32),
        compiler_params=_params(2),
        name="final_norm",
    )(s, g.reshape(1, d))


def _mm_resid_kernel(*refs, n_a, ctx_sub):
    a_refs = refs[:n_a]
    w_refs = refs[n_a:2 * n_a]
    r_ref, gb_ref, gc_ref, o_ref = refs[2 * n_a:]
    acc = _dot(a_refs[0][0], w_refs[0][...])
    for a_ref, w_ref in zip(a_refs[1:], w_refs[1:]):
        acc = acc + _dot(a_ref[0], w_ref[...])
    i = pl.program_id(2)
    n_sub = 1 if ctx_sub is None else acc.shape[0] // ROW_ALIGN
    sub = acc.shape[0] // n_sub
    for s in range(n_sub):
        gate = gb_ref[0]
        if ctx_sub is not None:
            gate = jnp.where(i * n_sub + s >= ctx_sub, gc_ref[0], gate)
        rows = slice(s * sub, (s + 1) * sub)
        o_ref[0, rows, :] = r_ref[0, rows, :] + gate * acc[rows, :]


def _mm_resid(a_list, w, resid, mod, k_gate, rows, tm, tn, ctx_rows_from, name):
    n_batch = resid.shape[0]
    n = w.shape[1]
    k = a_list[0].shape[2]
    assert all(a.shape[2] == k for a in a_list) and w.shape[0] == k * len(a_list)
    tm = _tile(rows, tm)
    ctx_sub = None if ctx_rows_from is None else ctx_rows_from // ROW_ALIGN
    ctx_row = n_batch
    w_spec = lambda part: pl.BlockSpec((k, tn), lambda j, b, i: (part, j))
    in_specs = [pl.BlockSpec((1, tm, k), lambda j, b, i: (b, i, 0)) for _ in a_list]
    in_specs += [w_spec(part) for part in range(len(a_list))]
    in_specs += [
        pl.BlockSpec((1, tm, tn), lambda j, b, i: (b, i, j)),
        pl.BlockSpec((1, 1, tn), lambda j, b, i: (b * 6 + k_gate, 0, j)),
        pl.BlockSpec((1, 1, tn), lambda j, b, i: (ctx_row * 6 + k_gate, 0, j)),
    ]
    return pl.pallas_call(
        functools.partial(_mm_resid_kernel, n_a=len(a_list), ctx_sub=ctx_sub),
        grid=(n // tn, n_batch, rows // tm),
        in_specs=in_specs,
        out_specs=pl.BlockSpec((1, tm, tn), lambda j, b, i: (b, i, j)),
        out_shape=jax.ShapeDtypeStruct((n_batch, rows, n), F32),
        compiler_params=_params(3),
        name=name,
    )(*a_list, *([w] * len(a_list)), resid, mod, mod)


def _axial_tables(n_lat, n_ctx, axis_dims):
    half = axis_dims // 2
    t = np.arange(n_lat)
    pos = np.stack([t // GRID_W, t % GRID_W], axis=1).astype(np.float64)
    inv = ROPE_THETA ** (-np.arange(half, dtype=np.float64) / half)
    lane = np.arange(HEAD_DIM)
    active = lane < 2 * axis_dims
    axis = np.where(active, lane // axis_dims, 0)
    within = lane % axis_dims
    ang = pos[:, axis] * inv[within % half][None, :]
    cos = np.where(active[None, :], np.cos(ang), 1.0)
    sin = np.where(active[None, :], np.sin(ang), 0.0)
    first = (within < half)[None, :]
    sin_up = np.where(first, -sin, 0.0)
    sin_dn = np.where(first, 0.0, sin)
    ident = np.zeros((n_ctx, HEAD_DIM))
    cos = np.concatenate([cos, ident + 1.0], axis=0)
    sin_up = np.concatenate([sin_up, ident], axis=0)
    sin_dn = np.concatenate([sin_dn, ident], axis=0)
    return tuple(jnp.asarray(v, dtype=F32) for v in (cos, sin_up, sin_dn))


def _rope128(v, cos, sin_up, sin_dn, half):
    return (v * cos + pltpu.roll(v, HEAD_DIM - half, 1) * sin_up
            + pltpu.roll(v, half, 1) * sin_dn)


LOG2E = 1.4426950408889634


def _softmax_pv(s, v_ones):
    p = jnp.exp2(s - jnp.max(s, axis=-1, keepdims=True))
    o = _dot(p.astype(BF16), v_ones)
    return o[:, :HEAD_DIM] / o[:, HEAD_DIM:HEAD_DIM + 1]


def _store_v_ones(vb_ref, v):
    vb_ref[:, :HEAD_DIM] = v
    vb_ref[:, HEAD_DIM:] = jnp.ones(v.shape, v.dtype)


def _gqa_kernel(q_ref, k_ref, v_ref, ck_ref, uk_ref, dk_ref,
                qg_ref, kg_ref, o_ref, kb_ref, vb_ref, *, n_lat, ctx_block):
    j = pl.program_id(2)
    half = HEAD_DIM // 4
    tq = q_ref.shape[1]
    q_rows = pl.ds(pl.multiple_of(j * tq, tq), tq)
    cq, uq, dq = ck_ref[q_rows, :], uk_ref[q_rows, :], dk_ref[q_rows, :]

    @pl.when(j == 0)
    def _():
        k = k_ref[0].astype(F32)
        k = k * lax.rsqrt(jnp.mean(k * k, axis=-1, keepdims=True) + NORM_EPS) * kg_ref[...]
        kb_ref[...] = _rope128(k, ck_ref[...], uk_ref[...], dk_ref[...], half).astype(BF16)
        _store_v_ones(vb_ref, v_ref[0])

    def attend(k, v):
        for g in range(GQA_GROUP):
            cols = slice(g * HEAD_DIM, (g + 1) * HEAD_DIM)
            q = q_ref[0, :, cols].astype(F32)
            q = q * lax.rsqrt(jnp.mean(q * q, axis=-1, keepdims=True) + NORM_EPS) * qg_ref[...]
            q = _rope128(q, cq, uq, dq, half) * (HEAD_DIM ** -0.5 * LOG2E)
            s = lax.dot_general(q.astype(BF16), k, (((1,), (1,)), ((), ())), preferred_element_type=F32)
            o_ref[0, :, cols] = _softmax_pv(s, v).astype(o_ref.dtype)

    @pl.when(j < ctx_block)
    def _():
        attend(kb_ref[...], vb_ref[...])

    @pl.when(j >= ctx_block)
    def _():
        attend(kb_ref[n_lat:, :], vb_ref[n_lat:, :])


def _gqa(z, tables, qn_g, kn_g, n_lat):
    n_batch, t, _ = z.shape
    tq = 256
    gw = GQA_GROUP * HEAD_DIM
    q0 = 2 * MIX_HALF // gw
    k0 = (2 * MIX_HALF + GQA_Q_HEADS * HEAD_DIM) // HEAD_DIM
    v0 = k0 + GQA_KV_HEADS
    tk_spec = pl.BlockSpec((t, HEAD_DIM), lambda b, h, j: (0, 0))
    g_spec = pl.BlockSpec((1, HEAD_DIM), lambda b, h, j: (0, 0))
    return pl.pallas_call(
        functools.partial(_gqa_kernel, n_lat=n_lat, ctx_block=n_lat // tq),
        grid=(n_batch, GQA_KV_HEADS, t // tq),
        in_specs=[
            pl.BlockSpec((1, tq, gw), lambda b, h, j: (b, j, q0 + h)),
            pl.BlockSpec((1, t, HEAD_DIM), lambda b, h, j: (b, 0, k0 + h)),
            pl.BlockSpec((1, t, HEAD_DIM), lambda b, h, j: (b, 0, v0 + h)),
            tk_spec, tk_spec, tk_spec, g_spec, g_spec,
        ],
        out_specs=pl.BlockSpec((1, tq, gw), lambda b, h, j: (b, j, h)),
        out_shape=jax.ShapeDtypeStruct((n_batch, t, GQA_Q_HEADS * HEAD_DIM), BF16),
        scratch_shapes=[pltpu.VMEM((t, HEAD_DIM), BF16), pltpu.VMEM((t, 2 * HEAD_DIM), BF16)],
        compiler_params=_params(3),
        name="gqa_attention",
    )(z, z, z, *tables, qn_g.reshape(1, HEAD_DIM), kn_g.reshape(1, HEAD_DIM))


CONV_ROWS = 32
CONV_HALO = 32
CONV_GAP = 16


def _conformer_kernel(a_ref, g_ref, w_ref, b_ref, lng_ref, lnb_ref, o_ref, u_ref, *, n_lat, n_ctx):
    ch = a_ref.shape[2]
    n_lane_blocks = ch // HEAD_DIM
    lat0 = CONV_GAP
    ctx0 = 2 * CONV_GAP + n_lat
    zeros = jnp.zeros((CONV_GAP, ch), F32)
    u_ref[0:CONV_GAP, :] = zeros
    u_ref[lat0 + n_lat:ctx0, :] = zeros
    u_ref[ctx0 + n_ctx:ctx0 + n_ctx + CONV_GAP, :] = zeros

    glu_rows = 256

    def glu(i, carry):
        src = pl.multiple_of(i * glu_rows, glu_rows)
        dst = pl.multiple_of(jnp.where(src >= n_lat, src + 2 * CONV_GAP, src + CONV_GAP), 8)
        a = a_ref[0, pl.ds(src, glu_rows), :].astype(F32)
        g = g_ref[0, pl.ds(src, glu_rows), :].astype(F32)
        u_ref[pl.ds(dst, glu_rows), :] = a * jax.nn.sigmoid(g)
        return carry

    lax.fori_loop(0, (n_lat + n_ctx) // glu_rows, glu, 0)

    win_rows = CONV_ROWS + CONV_HALO

    def conv_rows(i, carry):
        out_row = pl.multiple_of(i * CONV_ROWS, CONV_ROWS)
        win0 = pl.multiple_of(jnp.where(out_row >= n_lat, out_row + CONV_GAP, out_row), 8)
        accs = []
        for c in range(n_lane_blocks):
            lanes = slice(c * HEAD_DIM, (c + 1) * HEAD_DIM)
            win = u_ref[pl.ds(win0, win_rows), lanes]
            shifted = [win] + [pltpu.roll(win, win_rows - r, 0) for r in range(1, 8)]
            acc = jnp.zeros((CONV_ROWS, HEAD_DIM), F32)
            for k in range(CONV_WIDTH):
                off = k + CONV_GAP - CONV_PAD
                q, r = divmod(off, 8)
                acc = acc + w_ref[k:k + 1, lanes] * shifted[r][8 * q:8 * q + CONV_ROWS, :]
            accs.append(acc + b_ref[:, lanes])
        total = accs[0]
        for acc in accs[1:]:
            total = total + acc
        mu = jnp.sum(total, axis=-1, keepdims=True) * (1.0 / ch)
        cen = [acc - mu for acc in accs]
        sq = cen[0] * cen[0]
        for cc in cen[1:]:
            sq = sq + cc * cc
        rstd = lax.rsqrt(jnp.sum(sq, axis=-1, keepdims=True) * (1.0 / ch) + LN_EPS)
        for c in range(n_lane_blocks):
            lanes = slice(c * HEAD_DIM, (c + 1) * HEAD_DIM)
            y = cen[c] * rstd * lng_ref[:, lanes] + lnb_ref[:, lanes]
            o_ref[0, pl.ds(out_row, CONV_ROWS), lanes] = _silu(y).astype(o_ref.dtype)
        return carry

    lax.fori_loop(0, (n_lat + n_ctx) // CONV_ROWS, conv_rows, 0)


def _conformer(z, dw_w, dw_b, ln_g, ln_b, n_lat):
    n_batch, t, _ = z.shape
    ch = MIX_HALF
    n_ctx = t - n_lat
    pad_rows = t + 3 * CONV_GAP
    row = lambda v: v.reshape(1, ch)
    vec_spec = pl.BlockSpec((1, ch), lambda b: (0, 0))
    return pl.pallas_call(
        functools.partial(_conformer_kernel, n_lat=n_lat, n_ctx=n_ctx),
        grid=(n_batch,),
        in_specs=[
            pl.BlockSpec((1, t, ch), lambda b: (b, 0, 0)),
            pl.BlockSpec((1, t, ch), lambda b: (b, 0, 1)),
            pl.BlockSpec((CONV_WIDTH, ch), lambda b: (0, 0)),
            vec_spec, vec_spec, vec_spec,
        ],
        out_specs=pl.BlockSpec((1, t, ch), lambda b: (b, 0, 0)),
        out_shape=jax.ShapeDtypeStruct((n_batch, t, ch), BF16),
        scratch_shapes=[pltpu.VMEM((pad_rows, ch), F32)],
        compiler_params=_params(1),
        name="conformer_conv",
    )(z, z, dw_w, row(dw_b), row(ln_g), row(ln_b))


def _dft_matrix(n_lat):
    w = jnp.asarray(_dft_matrix_host(n_lat))
    return w.astype(BF16), w.T.astype(BF16)


@functools.lru_cache(maxsize=2)
def _dft_matrix_host(n_lat):
    m = 2 * n_lat
    r = np.arange(m)[:, None]
    t = np.arange(n_lat)[None, :]
    within = r % (2 * FREQ_TILE)
    is_sin = within >= FREQ_TILE
    f = (r // (2 * FREQ_TILE)) * FREQ_TILE + within % FREQ_TILE
    ang = ((f * t) % m) * (2.0 * math.pi / m)
    nyq = np.where(t % 2 == 0, 1.0, -1.0)
    return np.where(is_sin, np.where(f == 0, nyq, np.sin(ang)), np.cos(ang)).astype(np.float32)


def _filter_kernel(feat_ref, w1_ref, b1_ref, w2_ref, b2_ref, w3_ref, b3_ref, fr_ref, w4_ref,
                   t_ref, dl_ref, o_ref, h_ref, *, n_tiles):
    @pl.when(pl.program_id(0) == 0)
    def _():
        fr = fr_ref[...]
        h = jnp.sin(fr[0:1, :] * (_dot_hi(feat_ref[...], w1_ref[...]) + b1_ref[...]))
        h = jnp.sin(fr[1:2, :] * (_dot_hi(h, w2_ref[...]) + b2_ref[...]))
        h_ref[...] = jnp.sin(fr[2:3, :] * (_dot_hi(h, w3_ref[...]) + b3_ref[...]))

    hh = _dot_hi(h_ref[...], w4_ref[...]) * jnp.exp(-t_ref[...] * dl_ref[...])
    hh = hh / (jnp.sum(jnp.abs(hh), axis=0, keepdims=True) + 1e-6)
    is_bwd = pl.program_id(0) >= n_tiles // 2
    row = lax.broadcasted_iota(jnp.int32, hh.shape, 0)
    o_ref[...] = jnp.where(jnp.logical_and(is_bwd, row == 0), 0.0, hh).astype(o_ref.dtype)


def _hyena_filters(n_lat, w1, b1, w2, b2, w3, b3, w4, freq):
    fw = w1.shape[1]
    n = w4.shape[1]
    tn = 512
    bands = (HYENA_POS_DIM - 1) // 2
    t = np.linspace(0.0, 1.0, n_lat)[:, None]
    w = (2.0 * math.pi / n_lat) * np.arange(n_lat)[:, None]
    f = np.linspace(1e-4, bands - 1, bands)[None, :]
    feat = np.concatenate([t, np.cos(f * w), -np.sin(f * w),
                           np.zeros((n_lat, HYENA_POS_PAD - HYENA_POS_DIM))], axis=-1)
    deltas = np.abs(np.linspace(HYENA_MIN_DECAY, HYENA_MAX_DECAY, HYENA_CH))
    deltas = np.tile(deltas, n // HYENA_CH)[None, :]
    w1p = jnp.concatenate([w1, jnp.zeros((HYENA_POS_PAD - HYENA_POS_DIM, fw), F32)], axis=0)
    full = lambda shape: pl.BlockSpec(shape, lambda j: (0, 0))
    return pl.pallas_call(
        functools.partial(_filter_kernel, n_tiles=n // tn),
        grid=(n // tn,),
        in_specs=[
            full((n_lat, HYENA_POS_PAD)), full((HYENA_POS_PAD, fw)), full((1, fw)),
            full((fw, fw)), full((1, fw)), full((fw, fw)), full((1, fw)), full((3, fw)),
            pl.BlockSpec((fw, tn), lambda j: (0, j)),
            full((n_lat, 1)),
            pl.BlockSpec((1, tn), lambda j: (0, j)),
        ],
        out_specs=pl.BlockSpec((n_lat, tn), lambda j: (0, j)),
        out_shape=jax.ShapeDtypeStruct((n_lat, n), BF16),
        scratch_shapes=[pltpu.VMEM((n_lat, fw), F32)],
        compiler_params=_params(1),
        name="hyena_filters",
    )(jnp.asarray(feat, F32), w1p, b1.reshape(1, fw), w2, b2.reshape(1, fw), w3, b3.reshape(1, fw),
      freq, w4, jnp.asarray(t, F32), jnp.asarray(deltas, F32))


def _filter_dft_kernel(wf_ref, hf_ref, hb_ref, ka_ref, kb_ref, k2_ref, *, scale):
    gf = _dot(wf_ref[...], hf_ref[...])
    gb = _dot(wf_ref[...], hb_ref[...])
    kr = (gf[:FREQ_TILE] + gb[:FREQ_TILE]) * scale
    ki = (gb[FREQ_TILE:] - gf[FREQ_TILE:]) * scale
    nyq = (gf[FREQ_TILE:FREQ_TILE + 1] + gb[FREQ_TILE:FREQ_TILE + 1]) * scale
    row = lax.broadcasted_iota(jnp.int32, kr.shape, 0)
    first = jnp.logical_and(row == 0, pl.program_id(0) == 0)
    ka_ref[0] = jnp.where(first, 0.5 * kr, kr)
    kb_ref[0] = jnp.where(first, 0.5 * nyq, kr)
    k2_ref[0] = jnp.where(first, 0.0, ki)


def _filter_dft(wf, hmat, n_lat):
    tn = 512
    n_ct = HYENA_CH // tn
    out = jax.ShapeDtypeStruct((HYENA_ORDER, n_lat, HYENA_CH), F32)
    o_spec = pl.BlockSpec((1, FREQ_TILE, tn), lambda i, n, c: (n, i, c))
    return pl.pallas_call(
        functools.partial(_filter_dft_kernel, scale=1.0 / n_lat),
        grid=(n_lat // FREQ_TILE, HYENA_ORDER, n_ct),
        in_specs=[
            pl.BlockSpec((2 * FREQ_TILE, n_lat), lambda i, n, c: (i, 0)),
            pl.BlockSpec((n_lat, tn), lambda i, n, c: (0, n * n_ct + c)),
            pl.BlockSpec((n_lat, tn), lambda i, n, c: (0, (HYENA_ORDER + n) * n_ct + c)),
        ],
        out_specs=[o_spec, o_spec, o_spec],
        out_shape=[out, out, out],
        compiler_params=_params(3),
        name="hyena_filter_dft",
    )(wf, hmat, hmat)


def _short_conv_kernel(z_ref, w_ref, b_ref, o_ref):
    z = z_ref[0].astype(F32)
    n = z.shape[0]
    row = lax.broadcasted_iota(jnp.int32, z.shape, 0)
    prev = jnp.where(row == 0, 0.0, pltpu.roll(z, 1, 0))
    nxt = jnp.where(row == n - 1, 0.0, pltpu.roll(z, n - 1, 0))
    o = w_ref[0:1, :] * prev + w_ref[1:2, :] * z + w_ref[2:3, :] * nxt + b_ref[...]
    o_ref[0] = o.astype(o_ref.dtype)


def _short_conv(z, w, b, n_lat):
    n_batch = z.shape[0]
    n = w.shape[1]
    tc = 512
    return pl.pallas_call(
        _short_conv_kernel,
        grid=(n_batch, n // tc),
        in_specs=[pl.BlockSpec((1, n_lat, tc), lambda bb, c: (bb, 0, c)),
                  pl.BlockSpec((3, tc), lambda bb, c: (0, c)),
                  pl.BlockSpec((1, tc), lambda bb, c: (0, c))],
        out_specs=pl.BlockSpec((1, n_lat, tc), lambda bb, c: (bb, 0, c)),
        out_shape=jax.ShapeDtypeStruct((n_batch, n_lat, n), BF16),
        compiler_params=_params(2),
        name="hyena_short_conv",
    )(z, w, b.reshape(1, n))


def _spectrum_kernel(wf_ref, u_ref, ka_ref, kb_ref, k2_ref, o_ref):
    g = _dot(wf_ref[...], u_ref[0])
    p, q = g[:FREQ_TILE], g[FREQ_TILE:]
    o_ref[0, :FREQ_TILE, :] = (p * ka_ref[0] + q * k2_ref[0]).astype(o_ref.dtype)
    o_ref[0, FREQ_TILE:, :] = (q * kb_ref[0] - p * k2_ref[0]).astype(o_ref.dtype)


def _spectrum(wf, u, u_col0, spectra, order, n_lat):
    n_batch = u.shape[0]
    tn = HYENA_CH
    n_ct = HYENA_CH // tn
    c0 = u_col0 // tn
    k_spec = pl.BlockSpec((1, FREQ_TILE, tn), lambda i, c, b: (order, i, c))
    return pl.pallas_call(
        _spectrum_kernel,
        grid=(n_lat // FREQ_TILE, n_ct, n_batch),
        in_specs=[
            pl.BlockSpec((2 * FREQ_TILE, n_lat), lambda i, c, b: (i, 0)),
            pl.BlockSpec((1, n_lat, tn), lambda i, c, b: (b, 0, c0 + c)),
            k_spec, k_spec, k_spec,
        ],
        out_specs=pl.BlockSpec((1, 2 * FREQ_TILE, tn), lambda i, c, b: (b, i, c)),
        out_shape=jax.ShapeDtypeStruct((n_batch, 2 * n_lat, HYENA_CH), BF16),
        compiler_params=_params(3),
        name="hyena_spectrum",
    )(wf, u, *spectra)


def _inverse_gate_kernel(wt_ref, h_ref, u_ref, x_ref, skip_ref, o_ref):
    y = _dot(wt_ref[...], h_ref[0])
    u = u_ref[0].astype(F32)
    o_ref[0] = (x_ref[0].astype(F32) * (y + skip_ref[0] * u)).astype(o_ref.dtype)


def _inverse_gate(wft, h, u, u_col0, x, x_col0, skip, order, n_lat):
    n_batch = h.shape[0]
    tn = HYENA_CH
    tt = _tile(n_lat, 512)
    n_ct = HYENA_CH // tn
    uc0 = u_col0 // tn
    xc0 = x_col0 // tn
    return pl.pallas_call(
        _inverse_gate_kernel,
        grid=(n_lat // tt, n_ct, n_batch),
        in_specs=[
            pl.BlockSpec((tt, 2 * n_lat), lambda i, c, b: (i, 0)),
            pl.BlockSpec((1, 2 * n_lat, tn), lambda i, c, b: (b, 0, c)),
            pl.BlockSpec((1, tt, tn), lambda i, c, b: (b, i, uc0 + c)),
            pl.BlockSpec((1, tt, tn), lambda i, c, b: (b, i, xc0 + c)),
            pl.BlockSpec((1, 1, tn), lambda i, c, b: (order, 0, c)),
        ],
        out_specs=pl.BlockSpec((1, tt, tn), lambda i, c, b: (b, i, c)),
        out_shape=jax.ShapeDtypeStruct((n_batch, n_lat, HYENA_CH), BF16),
        compiler_params=_params(3),
        name="hyena_inverse_gate",
    )(wft, h, u, x, skip.reshape(HYENA_ORDER, 1, HYENA_CH))


def _hyena(z, short_w, short_b, filt, skip, n_lat):
    wf, wft = _dft_matrix(n_lat)
    hmat = _hyena_filters(n_lat, *filt)
    spectra = _filter_dft(wf, hmat, n_lat)
    zc = _short_conv(z, short_w, short_b, n_lat)
    h0 = _spectrum(wf, zc, 2 * HYENA_CH, spectra, 0, n_lat)
    y1 = _inverse_gate(wft, h0, zc, 2 * HYENA_CH, zc, 0, skip, 0, n_lat)
    h1 = _spectrum(wf, y1, 0, spectra, 1, n_lat)
    return _inverse_gate(wft, h1, y1, 0, zc, HYENA_CH, skip, 1, n_lat)


def _norm_mm_kernel(a_ref, g_ref, w_ref, o_ref):
    a = a_ref[0].astype(F32)
    a = a * lax.rsqrt(jnp.mean(a * a, axis=-1, keepdims=True) + NORM_EPS) * g_ref[...]
    o_ref[0] = _dot(a.astype(BF16), w_ref[...]).astype(o_ref.dtype)


def _norm_mm(z, col0, g, w, rows, tm, name):
    n_batch = z.shape[0]
    k, n = w.shape
    c0 = col0 // k
    tm = _tile(rows, tm)
    return pl.pallas_call(
        _norm_mm_kernel,
        grid=(n_batch, rows // tm),
        in_specs=[pl.BlockSpec((1, tm, k), lambda b, i: (b, i, c0)),
                  pl.BlockSpec((1, k), lambda b, i: (0, 0)),
                  pl.BlockSpec((k, n), lambda b, i: (0, 0))],
        out_specs=pl.BlockSpec((1, tm, n), lambda b, i: (b, i, 0)),
        out_shape=jax.ShapeDtypeStruct((n_batch, rows, n), BF16),
        compiler_params=_params(2),
        name=name,
    )(z, g.reshape(1, k), w)


MLA_ROWS = 512


def _mla_kernel(q_ref, kn_ref, v_ref, kr_ref, ck_ref, uk_ref, dk_ref, o_ref, kb_ref, vb_ref):
    j = pl.program_id(2)
    half = MLA_ROPE // 4

    @pl.when(j == 0)
    def _():
        kb_ref[:HEAD_DIM, :] = kn_ref[0].astype(F32).T.astype(BF16)
        kr = kr_ref[0].astype(F32)
        kb_ref[HEAD_DIM:, :] = _rope128(kr, ck_ref[...], uk_ref[...], dk_ref[...], half).T.astype(BF16)
        _store_v_ones(vb_ref, v_ref[0])

    scale = MLA_QK ** -0.5 * LOG2E
    tq = q_ref.shape[1]
    for r in range(tq // MLA_ROWS):
        rows = slice(r * MLA_ROWS, (r + 1) * MLA_ROWS)
        t_rows = pl.ds(pl.multiple_of(j * tq + r * MLA_ROWS, MLA_ROWS), MLA_ROWS)
        qn = (q_ref[0, rows, :HEAD_DIM].astype(F32) * scale).astype(BF16)
        qr = q_ref[0, rows, HEAD_DIM:].astype(F32)
        qr = (_rope128(qr, ck_ref[t_rows, :], uk_ref[t_rows, :], dk_ref[t_rows, :], half) * scale).astype(BF16)
        s = _dot(jnp.concatenate([qn, qr], axis=1), kb_ref[...])
        o_ref[0, rows, :] = _softmax_pv(s, vb_ref[...]).astype(o_ref.dtype)


def _mla(q, kv, z, tables, n_lat):
    n_batch, t, _ = kv.shape
    tq = _tile(n_lat, 2 * MLA_ROWS, MLA_ROWS)
    kr_block = (IN_ODD - MLA_ROPE) // HEAD_DIM
    tk_spec = pl.BlockSpec((t, HEAD_DIM), lambda b, h, j: (0, 0))
    return pl.pallas_call(
        _mla_kernel,
        grid=(n_batch, MLA_HEADS, n_lat // tq),
        in_specs=[
            pl.BlockSpec((1, tq, MLA_QK_PAD), lambda b, h, j: (b, j, h)),
            pl.BlockSpec((1, t, HEAD_DIM), lambda b, h, j: (b, 0, 2 * h)),
            pl.BlockSpec((1, t, HEAD_DIM), lambda b, h, j: (b, 0, 2 * h + 1)),
            pl.BlockSpec((1, t, HEAD_DIM), lambda b, h, j: (b, 0, kr_block)),
            tk_spec, tk_spec, tk_spec,
        ],
        out_specs=pl.BlockSpec((1, tq, HEAD_DIM), lambda b, h, j: (b, j, h)),
        out_shape=jax.ShapeDtypeStruct((n_batch, n_lat, MLA_HEADS * HEAD_DIM), BF16),
        scratch_shapes=[pltpu.VMEM((MLA_QK_PAD, t), BF16), pltpu.VMEM((t, 2 * HEAD_DIM), BF16)],
        compiler_params=_params(3),
        name="mla_attention",
    )(q, kv, kv, z, *tables)


def _pad_uq(w_uq):
    rank = w_uq.shape[0]
    w = w_uq.reshape(rank, MLA_HEADS, MLA_QK)
    w = jnp.concatenate([w, jnp.zeros((rank, MLA_HEADS, MLA_QK_PAD - MLA_QK), w.dtype)], axis=-1)
    return w.reshape(rank, MLA_HEADS * MLA_QK_PAD)


def _ffn(s, rows, norm_g, mod, wg, wu, wd, tm_resid, ctx_rows_from):
    act = _norm_matmul(s, rows, norm_g, mod, 3, 4, [wg, wu], 1024, 512, ctx_rows_from, "ffn_gate_up")
    return _mm_resid([act], wd, s, mod, 5, rows, tm_resid, 512, ctx_rows_from, "ffn_down")


def kernel(x, c, ctx, c_ctx, ada_w, ada_b, norm_mix_g, norm_ffn_g, e_w_in, e_w_out, e_dw_w, e_dw_b, e_ln_g, e_ln_b, e_qn_g, e_kn_g, o_w_in, o_w_out, o_short_w, o_short_b, o_f_w1, o_f_b1, o_f_w2, o_f_b2, o_f_w3, o_f_b3, o_f_w4, o_f_freq, o_skip, o_q_norm_g, o_kv_norm_g, o_w_uq, o_w_ukv, ffn_w_gate, ffn_w_up, ffn_w_down, final_norm_g):
    n_batch, n_lat, d = x.shape
    n_ctx = ctx.shape[1]
    t = n_lat + n_ctx
    bf = lambda w: w.astype(BF16)

    n_rows = -(-(n_batch + 1) // 8) * 8
    c_rows = jnp.concatenate(
        [c, c_ctx[None, :], jnp.zeros((n_rows - n_batch - 1, d), F32)], axis=0)
    mod = _ada(c_rows, ada_w, ada_b).reshape(ada_w.shape[0], n_rows * 6, 1, d)

    s = jnp.concatenate([x, ctx], axis=1)

    z = _norm_matmul(s, t, norm_mix_g[0], mod[0], 0, 1, [bf(e_w_in[0])], 1024, e_w_in.shape[2] // 2, n_lat,
                     "even_w_in")
    conv = _conformer(z, e_dw_w[0], e_dw_b[0], e_ln_g[0], e_ln_b[0], n_lat)
    att = _gqa(z, _axial_tables(n_lat, n_ctx, HEAD_DIM // 2), e_qn_g[0], e_kn_g[0], n_lat)
    s = _mm_resid([conv, att], bf(e_w_out[0]), s, mod[0], 2, t, 768, 1024,
                  n_lat, "even_w_out")
    s = _ffn(s, t, norm_ffn_g[0], mod[0], bf(ffn_w_gate[0]), bf(ffn_w_up[0]), bf(ffn_w_down[0]),
             768, n_lat)

    w_in = jnp.concatenate([o_w_in[0], jnp.zeros((d, IN_ODD_PAD - IN_ODD), F32)], axis=1)
    z = _norm_matmul(s, t, norm_mix_g[1], mod[1], 0, 1, [bf(w_in)], 1024, IN_ODD_PAD // 4, n_lat, "odd_w_in")
    filt = (o_f_w1[0], o_f_b1[0], o_f_w2[0], o_f_b2[0], o_f_w3[0], o_f_b3[0], o_f_w4[0], o_f_freq[0])
    y_h = _hyena(z, o_short_w[0], o_short_b[0], filt, o_skip[0], n_lat)
    q = _norm_mm(z, HYENA_IN, o_q_norm_g[0], bf(_pad_uq(o_w_uq[0])), n_lat, 1024, "mla_q_up")
    kv = _norm_mm(z, HYENA_IN + MLA_Q_RANK, o_kv_norm_g[0], bf(o_w_ukv[0]), t, 768, "mla_kv_up")
    att = _mla(q, kv, z, _axial_tables(n_lat, n_ctx, MLA_ROPE // 2), n_lat)
    xs = _mm_resid([y_h, att], bf(o_w_out[0]), s, mod[1], 2, n_lat, 1024, 1024,
                   None, "odd_w_out")
    xs = _ffn(xs, n_lat, norm_ffn_g[1], mod[1], bf(ffn_w_gate[1]), bf(ffn_w_up[1]), bf(ffn_w_down[1]),
              512, None)
    return _final_norm(xs, final_norm_g)
```

```python
import functools
import math

import numpy as np
import jax
import jax.numpy as jnp
from jax import lax
from jax.experimental import pallas as pl
from jax.experimental.pallas import tpu as pltpu

F32 = jnp.float32
BF16 = jnp.bfloat16

D_MODEL = 2048
GRID_W = 64
HEAD_DIM = 128
ROPE_THETA = 10000.0
NORM_EPS = 1e-6
LN_EPS = 1e-5
MIX_HALF = D_MODEL // 2
CONV_WIDTH = 31
CONV_PAD = (CONV_WIDTH - 1) // 2
GQA_Q_HEADS = MIX_HALF // HEAD_DIM
GQA_KV_HEADS = GQA_Q_HEADS // 4
GQA_GROUP = GQA_Q_HEADS // GQA_KV_HEADS
HYENA_CH = MIX_HALF
HYENA_ORDER = 2
HYENA_POS_DIM = 33
HYENA_POS_PAD = 40
HYENA_FAST_DECAY = 0.3
HYENA_SLOW_DECAY = 1.5
HYENA_TARGET = 1e-2
HYENA_MAX_DECAY = math.log(HYENA_TARGET) / HYENA_FAST_DECAY
HYENA_MIN_DECAY = math.log(HYENA_TARGET) / HYENA_SLOW_DECAY
HYENA_IN = 3 * HYENA_CH
MLA_HEADS = MIX_HALF // HEAD_DIM
MLA_Q_RANK = D_MODEL // 4
MLA_KV_RANK = D_MODEL // 8
MLA_NOPE = HEAD_DIM
MLA_ROPE = HEAD_DIM // 2
MLA_QK = MLA_NOPE + MLA_ROPE
MLA_QK_PAD = 2 * HEAD_DIM
IN_ODD = HYENA_IN + MLA_Q_RANK + MLA_KV_RANK + MLA_ROPE
IN_ODD_PAD = 4096

V7X_VMEM_LIMIT_BYTES = 56 * 1024 * 1024
FREQ_TILE = 512


def _params(n_axes):
    return pltpu.CompilerParams(
        dimension_semantics=("arbitrary",) * n_axes,
        vmem_limit_bytes=V7X_VMEM_LIMIT_BYTES,
    )


ROW_ALIGN = 256


def _tile(n, target, align=ROW_ALIGN):
    best = None
    for cand in range(align, min(n, target) + 1, align):
        if n % cand == 0:
            best = cand
    assert best is not None, (n, target, align)
    return best


def _dot(a, b):
    return jnp.dot(a, b, preferred_element_type=F32)


def _dot_hi(a, b):
    return jnp.dot(a, b, preferred_element_type=F32, precision=lax.Precision.HIGHEST)


def _silu(v):
    return v * jax.nn.sigmoid(v)


def _ada_kernel(c_ref, w_ref, b_ref, o_ref):
    o_ref[0] = _dot_hi(_silu(c_ref[...]), w_ref[0]) + b_ref[0]


def _ada(c_rows, ada_w, ada_b):
    depth, d, n = ada_w.shape
    r = c_rows.shape[0]
    tn = 512
    return pl.pallas_call(
        _ada_kernel,
        grid=(depth, n // tn),
        in_specs=[
            pl.BlockSpec((r, d), lambda l, j: (0, 0)),
            pl.BlockSpec((1, d, tn), lambda l, j: (l, 0, j)),
            pl.BlockSpec((1, 1, tn), lambda l, j: (l, 0, j)),
        ],
        out_specs=pl.BlockSpec((1, r, tn), lambda l, j: (l, 0, j)),
        out_shape=jax.ShapeDtypeStruct((depth, r, n), F32),
        compiler_params=_params(2),
        name="ada_mod",
    )(c_rows, ada_w, ada_b.reshape(depth, 1, n))


def _norm_matmul_kernel(x_ref, g_ref, shb_ref, scb_ref, shc_ref, scc_ref, *rest, ctx_sub, glu):
    if glu:
        wg_ref, wu_ref, o_ref, h_ref = rest
    else:
        w_ref, o_ref, h_ref = rest

    def load_weights():
        if glu:
            return wg_ref[...].astype(BF16), wu_ref[...].astype(BF16)
        return (w_ref[...].astype(BF16),)

    def project(h, ws):
        if glu:
            return (_silu(_dot(h, ws[0])) * _dot(h, ws[1])).astype(o_ref.dtype)
        return _dot(h, ws[0]).astype(o_ref.dtype)

    @pl.when(pl.program_id(2) == 0)
    def _():
        i = pl.program_id(1)
        n_sub = h_ref.shape[0] // ROW_ALIGN
        ws = load_weights()
        for s in range(n_sub):
            rows = slice(s * ROW_ALIGN, (s + 1) * ROW_ALIGN)
            x = x_ref[0, rows, :]
            y = x * lax.rsqrt(jnp.mean(x * x, axis=-1, keepdims=True) + NORM_EPS) * g_ref[...]
            shift, scale = shb_ref[0], scb_ref[0]
            if ctx_sub is not None:
                is_ctx = i * n_sub + s >= ctx_sub
                shift = jnp.where(is_ctx, shc_ref[0], shift)
                scale = jnp.where(is_ctx, scc_ref[0], scale)
            h = (y * (1.0 + scale) + shift).astype(h_ref.dtype)
            h_ref[rows, :] = h
            o_ref[0, rows, :] = project(h, ws)

    @pl.when(pl.program_id(2) > 0)
    def _():
        o_ref[0] = project(h_ref[...], load_weights())


def _norm_matmul(s, rows, g, mod, k_shift, k_scale, weights, tm, tn, ctx_rows_from, name, layer=None):
    n_batch, _, d = s.shape
    n = weights[0].shape[-1]
    tm = _tile(rows, tm)
    ctx_sub = None if ctx_rows_from is None else ctx_rows_from // ROW_ALIGN
    x_spec = pl.BlockSpec((1, tm, d), lambda b, i, j: (b, i, 0))
    batch_mod = lambda k: pl.BlockSpec((1, 1, d), lambda b, i, j: (b * 6 + k, 0, 0))
    ctx_mod = lambda k: pl.BlockSpec((1, 1, d), lambda b, i, j: (n_batch * 6 + k, 0, 0))
    if layer is None:
        w_spec = pl.BlockSpec((d, tn), lambda b, i, j: (0, j))
    else:
        w_spec = pl.BlockSpec((None, d, tn), lambda b, i, j: (layer, 0, j))
    return pl.pallas_call(
        functools.partial(_norm_matmul_kernel, ctx_sub=ctx_sub, glu=len(weights) == 2),
        grid=(n_batch, rows // tm, n // tn),
        in_specs=[x_spec, pl.BlockSpec((1, d), lambda b, i, j: (0, 0)),
                  batch_mod(k_shift), batch_mod(k_scale), ctx_mod(k_shift), ctx_mod(k_scale)]
                 + [w_spec] * len(weights),
        out_specs=pl.BlockSpec((1, tm, tn), lambda b, i, j: (b, i, j)),
        out_shape=jax.ShapeDtypeStruct((n_batch, rows, n), BF16),
        scratch_shapes=[pltpu.VMEM((tm, d), BF16)],
        compiler_params=_params(3),
        name=name,
    )(s, g.reshape(1, d), mod, mod, mod, mod, *weights)


def _final_norm_kernel(x_ref, g_ref, o_ref):
    x = x_ref[0]
    o_ref[0] = x * lax.rsqrt(jnp.mean(x * x, axis=-1, keepdims=True) + NORM_EPS) * g_ref[...]


def _final_norm(s, g):
    n_batch, rows, d = s.shape
    tr = 256
    return pl.pallas_call(
        _final_norm_kernel,
        grid=(n_batch, rows // tr),
        in_specs=[pl.BlockSpec((1, tr, d), lambda b, j: (b, j, 0)),
                  pl.BlockSpec((1, d), lambda b, j: (0, 0))],
        out_specs=pl.BlockSpec((1, tr, d), lambda b, j: (b, j, 0)),
        out_shape=jax.ShapeDtypeStruct((n_batch, rows, d), F32),
        compiler_params=_params(2),
        name="final_norm",
    )(s, g.reshape(1, d))


def _mm_resid_kernel(*refs, n_a, ctx_sub):
    a_refs = refs[:n_a]
    w_refs = refs[n_a:2 * n_a]
    r_ref, gb_ref, gc_ref, o_ref = refs[2 * n_a:]
    acc = _dot(a_refs[0][0], w_refs[0][...])
    for a_ref, w_ref in zip(a_refs[1:], w_refs[1:]):
        acc = acc + _dot(a_ref[0], w_ref[...])
    i = pl.program_id(2)
    n_sub = 1 if ctx_sub is None else acc.shape[0] // ROW_ALIGN
    sub = acc.shape[0] // n_sub
    for s in range(n_sub):
        gate = gb_ref[0]
        if ctx_sub is not None:
            gate = jnp.where(i * n_sub + s >= ctx_sub, gc_ref[0], gate)
        rows = slice(s * sub, (s + 1) * sub)
        o_ref[0, rows, :] = r_ref[0, rows, :] + gate * acc[rows, :]


def _mm_resid(a_list, w, resid, mod, k_gate, rows, tm, tn, ctx_rows_from, name):
    n_batch = resid.shape[0]
    n = w.shape[1]
    k = a_list[0].shape[2]
    assert all(a.shape[2] == k for a in a_list) and w.shape[0] == k * len(a_list)
    tm = _tile(rows, tm)
    ctx_sub = None if ctx_rows_from is None else ctx_rows_from // ROW_ALIGN
    ctx_row = n_batch
    w_spec = lambda part: pl.BlockSpec((k, tn), lambda j, b, i: (part, j))
    in_specs = [pl.BlockSpec((1, tm, k), lambda j, b, i: (b, i, 0)) for _ in a_list]
    in_specs += [w_spec(part) for part in range(len(a_list))]
    in_specs += [
        pl.BlockSpec((1, tm, tn), lambda j, b, i: (b, i, j)),
        pl.BlockSpec((1, 1, tn), lambda j, b, i: (b * 6 + k_gate, 0, j)),
        pl.BlockSpec((1, 1, tn), lambda j, b, i: (ctx_row * 6 + k_gate, 0, j)),
    ]
    return pl.pallas_call(
        functools.partial(_mm_resid_kernel, n_a=len(a_list), ctx_sub=ctx_sub),
        grid=(n // tn, n_batch, rows // tm),
        in_specs=in_specs,
        out_specs=pl.BlockSpec((1, tm, tn), lambda j, b, i: (b, i, j)),
        out_shape=jax.ShapeDtypeStruct((n_batch, rows, n), F32),
        compiler_params=_params(3),
        name=name,
    )(*a_list, *([w] * len(a_list)), resid, mod, mod)


def _axial_tables(n_lat, n_ctx, axis_dims):
    half = axis_dims // 2
    t = np.arange(n_lat)
    pos = np.stack([t // GRID_W, t % GRID_W], axis=1).astype(np.float64)
    inv = ROPE_THETA ** (-np.arange(half, dtype=np.float64) / half)
    lane = np.arange(HEAD_DIM)
    active = lane < 2 * axis_dims
    axis = np.where(active, lane // axis_dims, 0)
    within = lane % axis_dims
    ang = pos[:, axis] * inv[within % half][None, :]
    cos = np.where(active[None, :], np.cos(ang), 1.0)
    sin = np.where(active[None, :], np.sin(ang), 0.0)
    first = (within < half)[None, :]
    sin_up = np.where(first, -sin, 0.0)
    sin_dn = np.where(first, 0.0, sin)
    ident = np.zeros((n_ctx, HEAD_DIM))
    cos = np.concatenate([cos, ident + 1.0], axis=0)
    sin_up = np.concatenate([sin_up, ident], axis=0)
    sin_dn = np.concatenate([sin_dn, ident], axis=0)
    return tuple(jnp.asarray(v, dtype=F32) for v in (cos, sin_up, sin_dn))


def _rope128(v, cos, sin_up, sin_dn, half):
    return (v * cos + pltpu.roll(v, HEAD_DIM - half, 1) * sin_up
            + pltpu.roll(v, half, 1) * sin_dn)


LOG2E = 1.4426950408889634


def _softmax_pv(s, v_ones):
    p = jnp.exp2(s - jnp.max(s, axis=-1, keepdims=True))
    o = _dot(p.astype(BF16), v_ones)
    return o[:, :HEAD_DIM] / o[:, HEAD_DIM:HEAD_DIM + 1]


def _store_v_ones(vb_ref, v):
    vb_ref[:, :HEAD_DIM] = v
    vb_ref[:, HEAD_DIM:] = jnp.ones(v.shape, v.dtype)


def _gqa_kernel(q_ref, k_ref, v_ref, ck_ref, uk_ref, dk_ref,
                qg_ref, kg_ref, o_ref, kb_ref, vb_ref, *, n_lat, ctx_block):
    j = pl.program_id(2)
    half = HEAD_DIM // 4
    tq = q_ref.shape[1]
    q_rows = pl.ds(pl.multiple_of(j * tq, tq), tq)
    cq, uq, dq = ck_ref[q_rows, :], uk_ref[q_rows, :], dk_ref[q_rows, :]

    @pl.when(j == 0)
    def _():
        k = k_ref[0].astype(F32)
        k = k * lax.rsqrt(jnp.mean(k * k, axis=-1, keepdims=True) + NORM_EPS) * kg_ref[...]
        kb_ref[...] = _rope128(k, ck_ref[...], uk_ref[...], dk_ref[...], half).astype(BF16)
        _store_v_ones(vb_ref, v_ref[0])

    def attend(k, v):
        for g in range(GQA_GROUP):
            cols = slice(g * HEAD_DIM, (g + 1) * HEAD_DIM)
            q = q_ref[0, :, cols].astype(F32)
            q = q * lax.rsqrt(jnp.mean(q * q, axis=-1, keepdims=True) + NORM_EPS) * qg_ref[...]
            q = _rope128(q, cq, uq, dq, half) * (HEAD_DIM ** -0.5 * LOG2E)
            s = lax.dot_general(q.astype(BF16), k, (((1,), (1,)), ((), ())), preferred_element_type=F32)
            o_ref[0, :, cols] = _softmax_pv(s, v).astype(o_ref.dtype)

    @pl.when(j < ctx_block)
    def _():
        attend(kb_ref[...], vb_ref[...])

    @pl.when(j >= ctx_block)
    def _():
        attend(kb_ref[n_lat:, :], vb_ref[n_lat:, :])


def _gqa(z, tables, qn_g, kn_g, n_lat):
    n_batch, t, _ = z.shape
    tq = 256
    gw = GQA_GROUP * HEAD_DIM
    q0 = 2 * MIX_HALF // gw
    k0 = (2 * MIX_HALF + GQA_Q_HEADS * HEAD_DIM) // HEAD_DIM
    v0 = k0 + GQA_KV_HEADS
    tk_spec = pl.BlockSpec((t, HEAD_DIM), lambda b, h, j: (0, 0))
    g_spec = pl.BlockSpec((1, HEAD_DIM), lambda b, h, j: (0, 0))
    return pl.pallas_call(
        functools.partial(_gqa_kernel, n_lat=n_lat, ctx_block=n_lat // tq),
        grid=(n_batch, GQA_KV_HEADS, t // tq),
        in_specs=[
            pl.BlockSpec((1, tq, gw), lambda b, h, j: (b, j, q0 + h)),
            pl.BlockSpec((1, t, HEAD_DIM), lambda b, h, j: (b, 0, k0 + h)),
            pl.BlockSpec((1, t, HEAD_DIM), lambda b, h, j: (b, 0, v0 + h)),
            tk_spec, tk_spec, tk_spec, g_spec, g_spec,
        ],
        out_specs=pl.BlockSpec((1, tq, gw), lambda b, h, j: (b, j, h)),
        out_shape=jax.ShapeDtypeStruct((n_batch, t, GQA_Q_HEADS * HEAD_DIM), BF16),
        scratch_shapes=[pltpu.VMEM((t, HEAD_DIM), BF16), pltpu.VMEM((t, 2 * HEAD_DIM), BF16)],
        compiler_params=_params(3),
        name="gqa_attention",
    )(z, z, z, *tables, qn_g.reshape(1, HEAD_DIM), kn_g.reshape(1, HEAD_DIM))


CONV_ROWS = 32
CONV_HALO = 32
CONV_GAP = 16


def _conformer_kernel(a_ref, g_ref, w_ref, b_ref, lng_ref, lnb_ref, o_ref, u_ref, *, n_lat, n_ctx):
    ch = a_ref.shape[2]
    n_lane_blocks = ch // HEAD_DIM
    lat0 = CONV_GAP
    ctx0 = 2 * CONV_GAP + n_lat
    zeros = jnp.zeros((CONV_GAP, ch), F32)
    u_ref[0:CONV_GAP, :] = zeros
    u_ref[lat0 + n_lat:ctx0, :] = zeros
    u_ref[ctx0 + n_ctx:ctx0 + n_ctx + CONV_GAP, :] = zeros

    glu_rows = 256

    def glu(i, carry):
        src = pl.multiple_of(i * glu_rows, glu_rows)
        dst = pl.multiple_of(jnp.where(src >= n_lat, src + 2 * CONV_GAP, src + CONV_GAP), 8)
        a = a_ref[0, pl.ds(src, glu_rows), :].astype(F32)
        g = g_ref[0, pl.ds(src, glu_rows), :].astype(F32)
        u_ref[pl.ds(dst, glu_rows), :] = a * jax.nn.sigmoid(g)
        return carry

    lax.fori_loop(0, (n_lat + n_ctx) // glu_rows, glu, 0)

    win_rows = CONV_ROWS + CONV_HALO

    def conv_rows(i, carry):
        out_row = pl.multiple_of(i * CONV_ROWS, CONV_ROWS)
        win0 = pl.multiple_of(jnp.where(out_row >= n_lat, out_row + CONV_GAP, out_row), 8)
        accs = []
        for c in range(n_lane_blocks):
            lanes = slice(c * HEAD_DIM, (c + 1) * HEAD_DIM)
            win = u_ref[pl.ds(win0, win_rows), lanes]
            acc = jnp.zeros((CONV_ROWS, HEAD_DIM), F32)
            for r in range(8):
                shifted = win if r == 0 else pltpu.roll(win, win_rows - r, 0)
                for q in range(CONV_HALO // 8):
                    k = 8 * q + r - (CONV_GAP - CONV_PAD)
                    if 0 <= k < CONV_WIDTH:
                        acc = acc + w_ref[k:k + 1, lanes] * shifted[8 * q:8 * q + CONV_ROWS, :]
            accs.append(acc + b_ref[:, lanes])
        total = accs[0]
        for acc in accs[1:]:
            total = total + acc
        mu = jnp.sum(total, axis=-1, keepdims=True) * (1.0 / ch)
        cen = [acc - mu for acc in accs]
        sq = cen[0] * cen[0]
        for cc in cen[1:]:
            sq = sq + cc * cc
        rstd = lax.rsqrt(jnp.sum(sq, axis=-1, keepdims=True) * (1.0 / ch) + LN_EPS)
        for c in range(n_lane_blocks):
            lanes = slice(c * HEAD_DIM, (c + 1) * HEAD_DIM)
            y = cen[c] * rstd * lng_ref[:, lanes] + lnb_ref[:, lanes]
            o_ref[0, pl.ds(out_row, CONV_ROWS), lanes] = _silu(y).astype(o_ref.dtype)
        return carry

    lax.fori_loop(0, (n_lat + n_ctx) // CONV_ROWS, conv_rows, 0)


def _conformer(z, dw_w, dw_b, ln_g, ln_b, n_lat):
    n_batch, t, _ = z.shape
    ch = MIX_HALF
    n_ctx = t - n_lat
    pad_rows = t + 3 * CONV_GAP
    row = lambda v: v.reshape(1, ch)
    vec_spec = pl.BlockSpec((1, ch), lambda b: (0, 0))
    return pl.pallas_call(
        functools.partial(_conformer_kernel, n_lat=n_lat, n_ctx=n_ctx),
        grid=(n_batch,),
        in_specs=[
            pl.BlockSpec((1, t, ch), lambda b: (b, 0, 0)),
            pl.BlockSpec((1, t, ch), lambda b: (b, 0, 1)),
            pl.BlockSpec((CONV_WIDTH, ch), lambda b: (0, 0)),
            vec_spec, vec_spec, vec_spec,
        ],
        out_specs=pl.BlockSpec((1, t, ch), lambda b: (b, 0, 0)),
        out_shape=jax.ShapeDtypeStruct((n_batch, t, ch), BF16),
        scratch_shapes=[pltpu.VMEM((pad_rows, ch), F32)],
        compiler_params=_params(1),
        name="conformer_conv",
    )(z, z, dw_w, row(dw_b), row(ln_g), row(ln_b))


def _dft_matrix(n_lat):
    w = jnp.asarray(_dft_matrix_host(n_lat))
    return w.astype(BF16), w.T.astype(BF16)


@functools.lru_cache(maxsize=2)
def _dft_matrix_host(n_lat):
    m = 2 * n_lat
    r = np.arange(m)[:, None]
    t = np.arange(n_lat)[None, :]
    within = r % (2 * FREQ_TILE)
    is_sin = within >= FREQ_TILE
    f = (r // (2 * FREQ_TILE)) * FREQ_TILE + within % FREQ_TILE
    ang = ((f * t) % m) * (2.0 * math.pi / m)
    nyq = np.where(t % 2 == 0, 1.0, -1.0)
    return np.where(is_sin, np.where(f == 0, nyq, np.sin(ang)), np.cos(ang)).astype(np.float32)


def _filter_kernel(feat_ref, w1_ref, b1_ref, w2_ref, b2_ref, w3_ref, b3_ref, fr_ref, w4_ref,
                   t_ref, dl_ref, o_ref, h_ref, *, n_tiles):
    @pl.when(pl.program_id(0) == 0)
    def _():
        fr = fr_ref[...]
        h = jnp.sin(fr[0:1, :] * (_dot_hi(feat_ref[...], w1_ref[...]) + b1_ref[...]))
        h = jnp.sin(fr[1:2, :] * (_dot_hi(h, w2_ref[...]) + b2_ref[...]))
        h_ref[...] = jnp.sin(fr[2:3, :] * (_dot_hi(h, w3_ref[...]) + b3_ref[...]))

    hh = _dot_hi(h_ref[...], w4_ref[...]) * jnp.exp(-t_ref[...] * dl_ref[...])
    hh = hh / (jnp.sum(jnp.abs(hh), axis=0, keepdims=True) + 1e-6)
    is_bwd = pl.program_id(0) >= n_tiles // 2
    row = lax.broadcasted_iota(jnp.int32, hh.shape, 0)
    o_ref[...] = jnp.where(jnp.logical_and(is_bwd, row == 0), 0.0, hh).astype(o_ref.dtype)


def _hyena_filters(n_lat, w1, b1, w2, b2, w3, b3, w4, freq):
    fw = w1.shape[1]
    n = w4.shape[1]
    tn = 512
    bands = (HYENA_POS_DIM - 1) // 2
    t = np.linspace(0.0, 1.0, n_lat)[:, None]
    w = (2.0 * math.pi / n_lat) * np.arange(n_lat)[:, None]
    f = np.linspace(1e-4, bands - 1, bands)[None, :]
    feat = np.concatenate([t, np.cos(f * w), -np.sin(f * w),
                           np.zeros((n_lat, HYENA_POS_PAD - HYENA_POS_DIM))], axis=-1)
    deltas = np.abs(np.linspace(HYENA_MIN_DECAY, HYENA_MAX_DECAY, HYENA_CH))
    deltas = np.tile(deltas, n // HYENA_CH)[None, :]
    w1p = jnp.concatenate([w1, jnp.zeros((HYENA_POS_PAD - HYENA_POS_DIM, fw), F32)], axis=0)
    full = lambda shape: pl.BlockSpec(shape, lambda j: (0, 0))
    return pl.pallas_call(
        functools.partial(_filter_kernel, n_tiles=n // tn),
        grid=(n // tn,),
        in_specs=[
            full((n_lat, HYENA_POS_PAD)), full((HYENA_POS_PAD, fw)), full((1, fw)),
            full((fw, fw)), full((1, fw)), full((fw, fw)), full((1, fw)), full((3, fw)),
            pl.BlockSpec((fw, tn), lambda j: (0, j)),
            full((n_lat, 1)),
            pl.BlockSpec((1, tn), lambda j: (0, j)),
        ],
        out_specs=pl.BlockSpec((n_lat, tn), lambda j: (0, j)),
        out_shape=jax.ShapeDtypeStruct((n_lat, n), BF16),
        scratch_shapes=[pltpu.VMEM((n_lat, fw), F32)],
        compiler_params=_params(1),
        name="hyena_filters",
    )(jnp.asarray(feat, F32), w1p, b1.reshape(1, fw), w2, b2.reshape(1, fw), w3, b3.reshape(1, fw),
      freq, w4, jnp.asarray(t, F32), jnp.asarray(deltas, F32))


def _filter_dft_kernel(wf_ref, hf_ref, hb_ref, ka_ref, kb_ref, k2_ref, *, scale):
    gf = _dot(wf_ref[...], hf_ref[...])
    gb = _dot(wf_ref[...], hb_ref[...])
    kr = (gf[:FREQ_TILE] + gb[:FREQ_TILE]) * scale
    ki = (gb[FREQ_TILE:] - gf[FREQ_TILE:]) * scale
    nyq = (gf[FREQ_TILE:FREQ_TILE + 1] + gb[FREQ_TILE:FREQ_TILE + 1]) * scale
    row = lax.broadcasted_iota(jnp.int32, kr.shape, 0)
    first = jnp.logical_and(row == 0, pl.program_id(0) == 0)
    ka_ref[0] = jnp.where(first, 0.5 * kr, kr)
    kb_ref[0] = jnp.where(first, 0.5 * nyq, kr)
    k2_ref[0] = jnp.where(first, 0.0, ki)


def _filter_dft(wf, hmat, n_lat):
    tn = 512
    n_ct = HYENA_CH // tn
    out = jax.ShapeDtypeStruct((HYENA_ORDER, n_lat, HYENA_CH), F32)
    o_spec = pl.BlockSpec((1, FREQ_TILE, tn), lambda i, n, c: (n, i, c))
    return pl.pallas_call(
        functools.partial(_filter_dft_kernel, scale=1.0 / n_lat),
        grid=(n_lat // FREQ_TILE, HYENA_ORDER, n_ct),
        in_specs=[
            pl.BlockSpec((2 * FREQ_TILE, n_lat), lambda i, n, c: (i, 0)),
            pl.BlockSpec((n_lat, tn), lambda i, n, c: (0, n * n_ct + c)),
            pl.BlockSpec((n_lat, tn), lambda i, n, c: (0, (HYENA_ORDER + n) * n_ct + c)),
        ],
        out_specs=[o_spec, o_spec, o_spec],
        out_shape=[out, out, out],
        compiler_params=_params(3),
        name="hyena_filter_dft",
    )(wf, hmat, hmat)


def _short_conv_kernel(z_ref, w_ref, b_ref, o_ref):
    z = z_ref[0].astype(F32)
    n = z.shape[0]
    row = lax.broadcasted_iota(jnp.int32, z.shape, 0)
    prev = jnp.where(row == 0, 0.0, pltpu.roll(z, 1, 0))
    nxt = jnp.where(row == n - 1, 0.0, pltpu.roll(z, n - 1, 0))
    o = w_ref[0:1, :] * prev + w_ref[1:2, :] * z + w_ref[2:3, :] * nxt + b_ref[...]
    o_ref[0] = o.astype(o_ref.dtype)


def _short_conv(z, w, b, n_lat):
    n_batch = z.shape[0]
    n = w.shape[1]
    tc = 512
    return pl.pallas_call(
        _short_conv_kernel,
        grid=(n_batch, n // tc),
        in_specs=[pl.BlockSpec((1, n_lat, tc), lambda bb, c: (bb, 0, c)),
                  pl.BlockSpec((3, tc), lambda bb, c: (0, c)),
                  pl.BlockSpec((1, tc), lambda bb, c: (0, c))],
        out_specs=pl.BlockSpec((1, n_lat, tc), lambda bb, c: (bb, 0, c)),
        out_shape=jax.ShapeDtypeStruct((n_batch, n_lat, n), BF16),
        compiler_params=_params(2),
        name="hyena_short_conv",
    )(z, w, b.reshape(1, n))


def _spectrum_kernel(wf_ref, u_ref, ka_ref, kb_ref, k2_ref, o_ref):
    g = _dot(wf_ref[...], u_ref[0])
    p, q = g[:FREQ_TILE], g[FREQ_TILE:]
    o_ref[0, :FREQ_TILE, :] = (p * ka_ref[0] + q * k2_ref[0]).astype(o_ref.dtype)
    o_ref[0, FREQ_TILE:, :] = (q * kb_ref[0] - p * k2_ref[0]).astype(o_ref.dtype)


def _spectrum(wf, u, u_col0, spectra, order, n_lat):
    n_batch = u.shape[0]
    tn = HYENA_CH
    n_ct = HYENA_CH // tn
    c0 = u_col0 // tn
    k_spec = pl.BlockSpec((1, FREQ_TILE, tn), lambda i, c, b: (order, i, c))
    return pl.pallas_call(
        _spectrum_kernel,
        grid=(n_lat // FREQ_TILE, n_ct, n_batch),
        in_specs=[
            pl.BlockSpec((2 * FREQ_TILE, n_lat), lambda i, c, b: (i, 0)),
            pl.BlockSpec((1, n_lat, tn), lambda i, c, b: (b, 0, c0 + c)),
            k_spec, k_spec, k_spec,
        ],
        out_specs=pl.BlockSpec((1, 2 * FREQ_TILE, tn), lambda i, c, b: (b, i, c)),
        out_shape=jax.ShapeDtypeStruct((n_batch, 2 * n_lat, HYENA_CH), BF16),
        compiler_params=_params(3),
        name="hyena_spectrum",
    )(wf, u, *spectra)


def _inverse_gate_kernel(wt_ref, h_ref, u_ref, x_ref, skip_ref, o_ref):
    y = _dot(wt_ref[...], h_ref[0])
    u = u_ref[0].astype(F32)
    o_ref[0] = (x_ref[0].astype(F32) * (y + skip_ref[0] * u)).astype(o_ref.dtype)


def _inverse_gate(wft, h, u, u_col0, x, x_col0, skip, order, n_lat):
    n_batch = h.shape[0]
    tn = HYENA_CH
    tt = _tile(n_lat, 512)
    n_ct = HYENA_CH // tn
    uc0 = u_col0 // tn
    xc0 = x_col0 // tn
    return pl.pallas_call(
        _inverse_gate_kernel,
        grid=(n_lat // tt, n_ct, n_batch),
        in_specs=[
            pl.BlockSpec((tt, 2 * n_lat), lambda i, c, b: (i, 0)),
            pl.BlockSpec((1, 2 * n_lat, tn), lambda i, c, b: (b, 0, c)),
            pl.BlockSpec((1, tt, tn), lambda i, c, b: (b, i, uc0 + c)),
            pl.BlockSpec((1, tt, tn), lambda i, c, b: (b, i, xc0 + c)),
            pl.BlockSpec((1, 1, tn), lambda i, c, b: (order, 0, c)),
        ],
        out_specs=pl.BlockSpec((1, tt, tn), lambda i, c, b: (b, i, c)),
        out_shape=jax.ShapeDtypeStruct((n_batch, n_lat, HYENA_CH), BF16),
        compiler_params=_params(3),
        name="hyena_inverse_gate",
    )(wft, h, u, x, skip.reshape(HYENA_ORDER, 1, HYENA_CH))


def _hyena(z, short_w, short_b, filt, skip, n_lat):
    wf, wft = _dft_matrix(n_lat)
    hmat = _hyena_filters(n_lat, *filt)
    spectra = _filter_dft(wf, hmat, n_lat)
    zc = _short_conv(z, short_w, short_b, n_lat)
    h0 = _spectrum(wf, zc, 2 * HYENA_CH, spectra, 0, n_lat)
    y1 = _inverse_gate(wft, h0, zc, 2 * HYENA_CH, zc, 0, skip, 0, n_lat)
    h1 = _spectrum(wf, y1, 0, spectra, 1, n_lat)
    return _inverse_gate(wft, h1, y1, 0, zc, HYENA_CH, skip, 1, n_lat)


def _norm_mm_kernel(a_ref, g_ref, w_ref, o_ref):
    a = a_ref[0].astype(F32)
    a = a * lax.rsqrt(jnp.mean(a * a, axis=-1, keepdims=True) + NORM_EPS) * g_ref[...]
    o_ref[0] = _dot(a.astype(BF16), w_ref[...]).astype(o_ref.dtype)


def _norm_mm(z, col0, g, w, rows, tm, name):
    n_batch = z.shape[0]
    k, n = w.shape
    c0 = col0 // k
    tm = _tile(rows, tm)
    return pl.pallas_call(
        _norm_mm_kernel,
        grid=(n_batch, rows // tm),
        in_specs=[pl.BlockSpec((1, tm, k), lambda b, i: (b, i, c0)),
                  pl.BlockSpec((1, k), lambda b, i: (0, 0)),
                  pl.BlockSpec((k, n), lambda b, i: (0, 0))],
        out_specs=pl.BlockSpec((1, tm, n), lambda b, i: (b, i, 0)),
        out_shape=jax.ShapeDtypeStruct((n_batch, rows, n), BF16),
        compiler_params=_params(2),
        name=name,
    )(z, g.reshape(1, k), w)


MLA_ROWS = 512


def _mla_kernel(q_ref, kn_ref, v_ref, kr_ref, ck_ref, uk_ref, dk_ref, o_ref, kb_ref, vb_ref):
    j = pl.program_id(2)
    half = MLA_ROPE // 4

    @pl.when(j == 0)
    def _():
        kb_ref[:HEAD_DIM, :] = kn_ref[0].astype(F32).T.astype(BF16)
        kr = kr_ref[0].astype(F32)
        kb_ref[HEAD_DIM:, :] = _rope128(kr, ck_ref[...], uk_ref[...], dk_ref[...], half).T.astype(BF16)
        _store_v_ones(vb_ref, v_ref[0])

    scale = MLA_QK ** -0.5 * LOG2E
    tq = q_ref.shape[1]
    for r in range(tq // MLA_ROWS):
        rows = slice(r * MLA_ROWS, (r + 1) * MLA_ROWS)
        t_rows = pl.ds(pl.multiple_of(j * tq + r * MLA_ROWS, MLA_ROWS), MLA_ROWS)
        qn = (q_ref[0, rows, :HEAD_DIM].astype(F32) * scale).astype(BF16)
        qr = q_ref[0, rows, HEAD_DIM:].astype(F32)
        qr = (_rope128(qr, ck_ref[t_rows, :], uk_ref[t_rows, :], dk_ref[t_rows, :], half) * scale).astype(BF16)
        s = _dot(jnp.concatenate([qn, qr], axis=1), kb_ref[...])
        o_ref[0, rows, :] = _softmax_pv(s, vb_ref[...]).astype(o_ref.dtype)


def _mla(q, kv, z, tables, n_lat):
    n_batch, t, _ = kv.shape
    tq = _tile(n_lat, 2 * MLA_ROWS, MLA_ROWS)
    kr_block = (IN_ODD - MLA_ROPE) // HEAD_DIM
    tk_spec = pl.BlockSpec((t, HEAD_DIM), lambda b, h, j: (0, 0))
    return pl.pallas_call(
        _mla_kernel,
        grid=(n_batch, MLA_HEADS, n_lat // tq),
        in_specs=[
            pl.BlockSpec((1, tq, MLA_QK_PAD), lambda b, h, j: (b, j, h)),
            pl.BlockSpec((1, t, HEAD_DIM), lambda b, h, j: (b, 0, 2 * h)),
            pl.BlockSpec((1, t, HEAD_DIM), lambda b, h, j: (b, 0, 2 * h + 1)),
            pl.BlockSpec((1, t, HEAD_DIM), lambda b, h, j: (b, 0, kr_block)),
            tk_spec, tk_spec, tk_spec,
        ],
        out_specs=pl.BlockSpec((1, tq, HEAD_DIM), lambda b, h, j: (b, j, h)),
        out_shape=jax.ShapeDtypeStruct((n_batch, n_lat, MLA_HEADS * HEAD_DIM), BF16),
        scratch_shapes=[pltpu.VMEM((MLA_QK_PAD, t), BF16), pltpu.VMEM((t, 2 * HEAD_DIM), BF16)],
        compiler_params=_params(3),
        name="mla_attention",
    )(q, kv, kv, z, *tables)


def _pad_uq(w_uq):
    rank = w_uq.shape[0]
    w = w_uq.reshape(rank, MLA_HEADS, MLA_QK)
    w = jnp.concatenate([w, jnp.zeros((rank, MLA_HEADS, MLA_QK_PAD - MLA_QK), w.dtype)], axis=-1)
    return w.reshape(rank, MLA_HEADS * MLA_QK_PAD)


def _ffn(s, rows, norm_g, mod, layer, w_gate, w_up, wd, tm_resid, ctx_rows_from):
    act = _norm_matmul(s, rows, norm_g, mod, 3, 4, [w_gate, w_up], 1024, 512, ctx_rows_from, "ffn_gate_up",
                       layer=layer)
    return _mm_resid([act], wd, s, mod, 5, rows, tm_resid, 512, ctx_rows_from, "ffn_down")


def kernel(x, c, ctx, c_ctx, ada_w, ada_b, norm_mix_g, norm_ffn_g, e_w_in, e_w_out, e_dw_w, e_dw_b, e_ln_g, e_ln_b, e_qn_g, e_kn_g, o_w_in, o_w_out, o_short_w, o_short_b, o_f_w1, o_f_b1, o_f_w2, o_f_b2, o_f_w3, o_f_b3, o_f_w4, o_f_freq, o_skip, o_q_norm_g, o_kv_norm_g, o_w_uq, o_w_ukv, ffn_w_gate, ffn_w_up, ffn_w_down, final_norm_g):
    n_batch, n_lat, d = x.shape
    n_ctx = ctx.shape[1]
    t = n_lat + n_ctx
    bf = lambda w: w.astype(BF16)

    n_rows = -(-(n_batch + 1) // 8) * 8
    c_rows = jnp.concatenate(
        [c, c_ctx[None, :], jnp.zeros((n_rows - n_batch - 1, d), F32)], axis=0)
    mod = _ada(c_rows, ada_w, ada_b).reshape(ada_w.shape[0], n_rows * 6, 1, d)

    s = jnp.concatenate([x, ctx], axis=1)

    z = _norm_matmul(s, t, norm_mix_g[0], mod[0], 0, 1, [bf(e_w_in[0])], 1024, e_w_in.shape[2] // 2, n_lat,
                     "even_w_in")
    conv = _conformer(z, e_dw_w[0], e_dw_b[0], e_ln_g[0], e_ln_b[0], n_lat)
    att = _gqa(z, _axial_tables(n_lat, n_ctx, HEAD_DIM // 2), e_qn_g[0], e_kn_g[0], n_lat)
    s = _mm_resid([conv, att], bf(e_w_out[0]), s, mod[0], 2, t, 768, 1024,
                  n_lat, "even_w_out")
    s = _ffn(s, t, norm_ffn_g[0], mod[0], 0, ffn_w_gate, ffn_w_up, bf(ffn_w_down[0]),
             768, n_lat)

    w_in = jnp.concatenate([o_w_in[0], jnp.zeros((d, IN_ODD_PAD - IN_ODD), F32)], axis=1)
    z = _norm_matmul(s, t, norm_mix_g[1], mod[1], 0, 1, [bf(w_in)], 1024, IN_ODD_PAD // 4, n_lat, "odd_w_in")
    filt = (o_f_w1[0], o_f_b1[0], o_f_w2[0], o_f_b2[0], o_f_w3[0], o_f_b3[0], o_f_w4[0], o_f_freq[0])
    y_h = _hyena(z, o_short_w[0], o_short_b[0], filt, o_skip[0], n_lat)
    q = _norm_mm(z, HYENA_IN, o_q_norm_g[0], bf(_pad_uq(o_w_uq[0])), n_lat, 1024, "mla_q_up")
    kv = _norm_mm(z, HYENA_IN + MLA_Q_RANK, o_kv_norm_g[0], bf(o_w_ukv[0]), t, 768, "mla_kv_up")
    att = _mla(q, kv, z, _axial_tables(n_lat, n_ctx, MLA_ROPE // 2), n_lat)
    xs = _mm_resid([y_h, att], bf(o_w_out[0]), s, mod[1], 2, n_lat, 1024, 1024,
                   None, "odd_w_out")
    xs = _ffn(xs, n_lat, norm_ffn_g[1], mod[1], 1, ffn_w_gate, ffn_w_up, bf(ffn_w_down[1]),
              512, None)
    return _final_norm(xs, final_norm_g)
```
